```python
import math
import jax
import jax.numpy as jnp
from jax import lax
import numpy as np

D_MODEL = 2048
BATCH = 4
SEQ = 4096
DEPTH = 2
DEC_BATCH = 128
DEC_SEQ = 8
PAST_LEN = 16384
PAGE_SIZE = 128

N_MIXERS = 2
N_MLA_LAYERS = (DEPTH + 1) // 2
N_SWA_LAYERS = DEPTH // 2
MLA_HEADS = 16
Q_LORA = 512
KV_LORA = 512
NOPE_DIM = 128
ROPE_DIM = 64
V_DIM = 128
ROPE_BASE = 10000.0
Q_BLOCK = 128
SWA_HEADS = 32
SWA_KV_HEADS = 8
SWA_GROUP = SWA_HEADS // SWA_KV_HEADS
SWA_HEAD_DIM = 64
WINDOW = 128
REL_BUCKETS = 32
REL_MAX_DIST = 128
D_FF = 4 * D_MODEL
PLE_DIM = 256
EPS = 1e-6

kernel_name = "mla_swa_sink_hybrid_step"


def rmsnorm(x, g):
    xf = x.astype(jnp.float32)
    y = xf * lax.rsqrt(jnp.mean(xf * xf, axis=-1, keepdims=True) + EPS)
    return (y * g.astype(jnp.float32)).astype(x.dtype)


def rope(x, pos):
    half = ROPE_DIM // 2
    inv = ROPE_BASE ** (-jnp.arange(half, dtype=jnp.float32) / half)
    ang = pos.astype(jnp.float32)[:, None] * inv[None, :]
    shp = (pos.shape[0],) + (1,) * (x.ndim - 3) + (half,)
    cos = jnp.cos(ang).reshape(shp)
    sin = jnp.sin(ang).reshape(shp)
    xf = x.astype(jnp.float32)
    x1, x2 = xf[..., :half], xf[..., half:]
    return jnp.concatenate([x1 * cos - x2 * sin, x1 * sin + x2 * cos], axis=-1).astype(x.dtype)


def t5_bucket(dist):
    n = jnp.maximum(dist, 0)
    exact = REL_BUCKETS // 2
    nf = jnp.maximum(n, 1).astype(jnp.float32)
    large = exact + (jnp.log(nf / exact) / math.log(REL_MAX_DIST / exact)
                     * (REL_BUCKETS - exact)).astype(jnp.int32)
    large = jnp.minimum(large, REL_BUCKETS - 1)
    return jnp.where(n < exact, n, large)


def rel_bias_heads(dist, rel_table):
    b = rel_table[t5_bucket(dist)].astype(jnp.float32)
    b = b.reshape(dist.shape + (SWA_KV_HEADS, SWA_GROUP))
    return jnp.transpose(b, (2, 3, 0, 1))


def sink_softmax(s, sinks):
    sk = sinks.astype(jnp.float32).reshape(SWA_KV_HEADS, SWA_GROUP, 1, 1)
    m = jnp.maximum(jnp.max(s, axis=-1, keepdims=True), sk)
    e = jnp.exp(s - m)
    return e / (jnp.sum(e, axis=-1, keepdims=True) + jnp.exp(sk - m))


def mla_project(h, pos, w_in, g_q, g_kv, w_qb):
    B, S, _ = h.shape
    z = h @ w_in
    cq = rmsnorm(z[..., :Q_LORA], g_q)
    ckv = rmsnorm(z[..., Q_LORA:Q_LORA + KV_LORA], g_kv)
    k_rope = rope(z[..., Q_LORA + KV_LORA:], pos)
    q = (cq @ w_qb).reshape(B, S, MLA_HEADS, NOPE_DIM + ROPE_DIM)
    q_nope = q[..., :NOPE_DIM]
    q_rope = rope(q[..., NOPE_DIM:], pos)
    return q_nope, q_rope, ckv, k_rope


def mla_prompt(h, w_in, g_q, g_kv, w_qb, w_kvb, w_o):
    B, S, _ = h.shape
    pos = jnp.arange(S)
    q_nope, q_rope, ckv, k_rope = mla_project(h, pos, w_in, g_q, g_kv, w_qb)
    kv = (ckv @ w_kvb).reshape(B, S, MLA_HEADS, NOPE_DIM + V_DIM)
    k_nope, v = kv[..., :NOPE_DIM], kv[..., NOPE_DIM:]
    scale = (NOPE_DIM + ROPE_DIM) ** -0.5
    nb = S // Q_BLOCK

    def block(args):
        qn, qr, q0 = args
        s = (jnp.einsum('bqhd,bshd->bhqs', qn, k_nope)
             + jnp.einsum('bqhr,bsr->bhqs', qr, k_rope)).astype(jnp.float32) * scale
        qpos = q0 + jnp.arange(Q_BLOCK)
        s = jnp.where(pos[None, :] <= qpos[:, None], s, -jnp.inf)
        p = jax.nn.softmax(s, axis=-1).astype(v.dtype)
        return jnp.einsum('bhqs,bshd->bqhd', p, v)

    qn_b = jnp.moveaxis(q_nope.reshape(B, nb, Q_BLOCK, MLA_HEADS, NOPE_DIM), 1, 0)
    qr_b = jnp.moveaxis(q_rope.reshape(B, nb, Q_BLOCK, MLA_HEADS, ROPE_DIM), 1, 0)
    o = lax.map(block, (qn_b, qr_b, jnp.arange(nb, dtype=jnp.int32) * Q_BLOCK))
    o = jnp.moveaxis(o, 0, 1).reshape(B, S, MLA_HEADS * V_DIM)
    return o @ w_o, ckv, k_rope


def mla_sample(h, lat_pool, rope_pool, layer, page_table, w_in, g_q, g_kv, w_qb, w_kvb, w_o):
    Bd, T, _ = h.shape
    past = page_table.shape[1] * PAGE_SIZE
    pos = past + jnp.arange(T)
    q_nope, q_rope, ckv, k_rope = mla_project(h, pos, w_in, g_q, g_kv, w_qb)
    w_kv = w_kvb.reshape(KV_LORA, MLA_HEADS, NOPE_DIM + V_DIM)
    w_uk, w_uv = w_kv[..., :NOPE_DIM], w_kv[..., NOPE_DIM:]
    q_lat = jnp.einsum('bthd,chd->bthc', q_nope, w_uk)
    scale = (NOPE_DIM + ROPE_DIM) ** -0.5
    causal = jnp.arange(T)[None, :] <= jnp.arange(T)[:, None]

    def one(args):
        ql, qr, c_new, r_new, pages = args
        c_past = lat_pool[layer, pages].reshape(past, KV_LORA)
        r_past = rope_pool[layer, pages].reshape(past, ROPE_DIM)
        s_past = (jnp.einsum('thc,sc->hts', ql, c_past)
                  + jnp.einsum('thr,sr->hts', qr, r_past)).astype(jnp.float32) * scale
        s_new = (jnp.einsum('thc,uc->htu', ql, c_new)
                 + jnp.einsum('thr,ur->htu', qr, r_new)).astype(jnp.float32) * scale
        s_new = jnp.where(causal[None], s_new, -jnp.inf)
        p = jax.nn.softmax(jnp.concatenate([s_past, s_new], axis=-1), axis=-1).astype(c_new.dtype)
        return (jnp.einsum('hts,sc->thc', p[..., :past], c_past)
                + jnp.einsum('htu,uc->thc', p[..., past:], c_new))

    o_lat = lax.map(one, (q_lat, q_rope, ckv, k_rope, page_table))
    o = jnp.einsum('bthc,chv->bthv', o_lat, w_uv).reshape(Bd, T, MLA_HEADS * V_DIM)
    return o @ w_o, ckv, k_rope


def swa_project(h, w_qkv, b_qkv):
    B, S, _ = h.shape
    z = h @ w_qkv + b_qkv
    nq = SWA_HEADS * SWA_HEAD_DIM
    nk = SWA_KV_HEADS * SWA_HEAD_DIM
    q = z[..., :nq].reshape(B, S, SWA_KV_HEADS, SWA_GROUP, SWA_HEAD_DIM)
    k = z[..., nq:nq + nk].reshape(B, S, SWA_KV_HEADS, SWA_HEAD_DIM)
    v = z[..., nq + nk:].reshape(B, S, SWA_KV_HEADS, SWA_HEAD_DIM)
    return q, k, v


def swa_prompt(h, w_qkv, b_qkv, sinks, rel_table, w_o):
    B, S, _ = h.shape
    q, k, v = swa_project(h, w_qkv, b_qkv)
    nb = S // WINDOW
    scale = SWA_HEAD_DIM ** -0.5
    qb = q.reshape(B, nb, WINDOW, SWA_KV_HEADS, SWA_GROUP, SWA_HEAD_DIM)

    def band(t):
        tc = t.reshape(B, nb, WINDOW, SWA_KV_HEADS, SWA_HEAD_DIM)
        prev = jnp.pad(tc, ((0, 0), (1, 0), (0, 0), (0, 0), (0, 0)))[:, :-1]
        return jnp.concatenate([prev, tc], axis=2)

    kb, vb = band(k), band(v)
    qi = jnp.arange(WINDOW)
    kj = jnp.arange(2 * WINDOW)
    dist = (WINDOW + qi[:, None]) - kj[None, :]
    key_pos = (jnp.arange(nb) * WINDOW - WINDOW)[:, None, None] + kj[None, None, :]
    mask = (dist >= 0)[None] & (dist < WINDOW)[None] & (key_pos >= 0)
    bias = rel_bias_heads(dist, rel_table)
    s = jnp.einsum('bnqkgd,bnskd->bnkgqs', qb, kb).astype(jnp.float32) * scale + bias
    s = jnp.where(mask[None, :, None, None], s, -jnp.inf)
    p = sink_softmax(s, sinks).astype(vb.dtype)
    o = jnp.einsum('bnkgqs,bnskd->bnqkgd', p, vb).reshape(B, S, SWA_HEADS * SWA_HEAD_DIM)
    return o @ w_o, k[:, -WINDOW:], v[:, -WINDOW:]


def swa_sample(h, buf_k, buf_v, w_qkv, b_qkv, sinks, rel_table, w_o):
    Bd, T, _ = h.shape
    nbuf = buf_k.shape[1]
    q, k, v = swa_project(h, w_qkv, b_qkv)
    kk = jnp.concatenate([buf_k, k], axis=1)
    vv = jnp.concatenate([buf_v, v], axis=1)
    scale = SWA_HEAD_DIM ** -0.5
    dist = (nbuf + jnp.arange(T))[:, None] - jnp.arange(nbuf + T)[None, :]
    mask = (dist >= 0) & (dist < WINDOW)
    bias = rel_bias_heads(dist, rel_table)
    s = jnp.einsum('btkgd,bskd->bkgts', q, kk).astype(jnp.float32) * scale + bias
    s = jnp.where(mask[None, None, None], s, -jnp.inf)
    p = sink_softmax(s, sinks).astype(vv.dtype)
    o = jnp.einsum('bkgts,bskd->btkgd', p, vv).reshape(Bd, T, SWA_HEADS * SWA_HEAD_DIM)
    return o @ w_o, kk[:, -nbuf:], vv[:, -nbuf:]


def channel_and_ple(x, p_i, g_mlp, w1, w2, g_ple, w_gate, w_proj):
    hm = rmsnorm(x, g_mlp)
    x = x + jnp.square(jax.nn.relu(hm @ w1)) @ w2
    gate = jax.nn.sigmoid(rmsnorm(x, g_ple) @ w_gate)
    return x + gate * (p_i @ w_proj)


def setup_inputs(seed: int = 0) -> dict:
    key = jax.random.key(seed)
    ks = jax.random.split(key, 32)
    f32 = jnp.float32

    def nrm(k, shape, scale=1.0):
        return jax.random.normal(k, shape, f32) * scale

    def gain(k, shape):
        return 1.0 + 0.05 * jax.random.normal(k, shape, f32)

    n_pages = PAST_LEN // PAGE_SIZE
    n_pool = (DEC_BATCH * n_pages * 5) // 4
    nbuf = min(WINDOW, PAST_LEN)
    qkv_w = (SWA_HEADS + 2 * SWA_KV_HEADS) * SWA_HEAD_DIM
    page_table = jax.random.permutation(ks[6], n_pool)[:DEC_BATCH * n_pages]
    page_table = page_table.reshape(DEC_BATCH, n_pages).astype(jnp.int32)
    return {
        "x_prompt": nrm(ks[0], (BATCH, SEQ, D_MODEL)),
        "x_sample": nrm(ks[1], (DEC_BATCH, DEC_SEQ, D_MODEL)),
        "cache_mla_latent": nrm(ks[2], (N_MLA_LAYERS, n_pool, PAGE_SIZE, KV_LORA)),
        "cache_mla_rope": nrm(ks[3], (N_MLA_LAYERS, n_pool, PAGE_SIZE, ROPE_DIM)),
        "cache_swa_k": nrm(ks[4], (N_SWA_LAYERS, DEC_BATCH, nbuf, SWA_KV_HEADS, SWA_HEAD_DIM)),
        "cache_swa_v": nrm(ks[5], (N_SWA_LAYERS, DEC_BATCH, nbuf, SWA_KV_HEADS, SWA_HEAD_DIM)),
        "page_table": page_table,
        "p_prompt": nrm(ks[7], (DEPTH, BATCH, SEQ, PLE_DIM)),
        "p_sample": nrm(ks[8], (DEPTH, DEC_BATCH, DEC_SEQ, PLE_DIM)),
        "norm_mix": gain(ks[9], (DEPTH, D_MODEL)),
        "norm_mlp": gain(ks[10], (DEPTH, D_MODEL)),
        "norm_ple": gain(ks[11], (DEPTH, D_MODEL)),
        "norm_final": gain(ks[12], (D_MODEL,)),
        "mla_w_in": nrm(ks[13], (N_MLA_LAYERS, D_MODEL, Q_LORA + KV_LORA + ROPE_DIM), D_MODEL ** -0.5),
        "mla_g_q": gain(ks[14], (N_MLA_LAYERS, Q_LORA)),
        "mla_g_kv": gain(ks[15], (N_MLA_LAYERS, KV_LORA)),
        "mla_w_qb": nrm(ks[16], (N_MLA_LAYERS, Q_LORA, MLA_HEADS * (NOPE_DIM + ROPE_DIM)), Q_LORA ** -0.5),
        "mla_w_kvb": nrm(ks[17], (N_MLA_LAYERS, KV_LORA, MLA_HEADS * (NOPE_DIM + V_DIM)), KV_LORA ** -0.5),
        "mla_w_o": nrm(ks[18], (N_MLA_LAYERS, MLA_HEADS * V_DIM, D_MODEL), (MLA_HEADS * V_DIM) ** -0.5),
        "swa_w_qkv": nrm(ks[19], (N_SWA_LAYERS, D_MODEL, qkv_w), D_MODEL ** -0.5),
        "swa_b_qkv": nrm(ks[20], (N_SWA_LAYERS, qkv_w), 0.02),
        "swa_sinks": nrm(ks[21], (N_SWA_LAYERS, SWA_HEADS), 1.0),
        "swa_w_o": nrm(ks[22], (N_SWA_LAYERS, SWA_HEADS * SWA_HEAD_DIM, D_MODEL), (SWA_HEADS * SWA_HEAD_DIM) ** -0.5),
        "rel_bias": nrm(ks[23], (REL_BUCKETS, SWA_HEADS), 0.5),
        "mlp_w1": nrm(ks[24], (DEPTH, D_MODEL, D_FF), D_MODEL ** -0.5),
        "mlp_w2": nrm(ks[25], (DEPTH, D_FF, D_MODEL), D_FF ** -0.5),
        "ple_w_gate": nrm(ks[26], (DEPTH, D_MODEL, D_MODEL), D_MODEL ** -0.5),
        "ple_w_proj": nrm(ks[27], (DEPTH, PLE_DIM, D_MODEL), PLE_DIM ** -0.5),
    }


def reference(x_prompt, x_sample, cache_mla_latent, cache_mla_rope, cache_swa_k, cache_swa_v,
              page_table, p_prompt, p_sample,
              norm_mix, norm_mlp, norm_ple, norm_final,
              mla_w_in, mla_g_q, mla_g_kv, mla_w_qb, mla_w_kvb, mla_w_o,
              swa_w_qkv, swa_b_qkv, swa_sinks, swa_w_o, rel_bias,
              mlp_w1, mlp_w2, ple_w_gate, ple_w_proj):
    xp, xs = x_prompt, x_sample
    lat_p, rope_p, lat_s, rope_s = [], [], [], []
    kp, vp, ksm, vsm = [], [], [], []
    for i in range(DEPTH):
        j = i // N_MIXERS
        hp = rmsnorm(xp, norm_mix[i])
        hs = rmsnorm(xs, norm_mix[i])
        if i % N_MIXERS == 0:
            mw = (mla_w_in[j], mla_g_q[j], mla_g_kv[j], mla_w_qb[j], mla_w_kvb[j], mla_w_o[j])
            op, c_p, r_p = mla_prompt(hp, *mw)
            osm, c_s, r_s = mla_sample(hs, cache_mla_latent, cache_mla_rope, j, page_table, *mw)
            lat_p.append(c_p)
            rope_p.append(r_p)
            lat_s.append(c_s)
            rope_s.append(r_s)
        else:
            sw = (swa_w_qkv[j], swa_b_qkv[j], swa_sinks[j], rel_bias, swa_w_o[j])
            op, k_p, v_p = swa_prompt(hp, *sw)
            osm, k_s, v_s = swa_sample(hs, cache_swa_k[j], cache_swa_v[j], *sw)
            kp.append(k_p)
            vp.append(v_p)
            ksm.append(k_s)
            vsm.append(v_s)
        xp = xp + op
        xs = xs + osm
        cw = (norm_mlp[i], mlp_w1[i], mlp_w2[i], norm_ple[i], ple_w_gate[i], ple_w_proj[i])
        xp = channel_and_ple(xp, p_prompt[i], *cw)
        xs = channel_and_ple(xs, p_sample[i], *cw)
    y_prompt = rmsnorm(xp, norm_final)
    y_sample = rmsnorm(xs, norm_final)
    return (y_prompt, y_sample,
            jnp.stack(lat_p), jnp.stack(rope_p), jnp.stack(lat_s), jnp.stack(rope_s),
            jnp.stack(kp), jnp.stack(vp), jnp.stack(ksm), jnp.stack(vsm))
```

```python
import functools
import math

import numpy as np
import jax
import jax.numpy as jnp
from jax import lax
from jax.experimental import pallas as pl
from jax.experimental.pallas import tpu as pltpu

F32 = jnp.float32
BF16 = jnp.bfloat16

MLA_HEADS = 16
NOPE_DIM = 128
ROPE_DIM = 64
V_DIM = 128
ROPE_BASE = 10000.0
SWA_HEADS = 32
SWA_KV_HEADS = 8
SWA_GROUP = SWA_HEADS // SWA_KV_HEADS
SWA_HEAD_DIM = 64
WINDOW = 128
REL_BUCKETS = 32
REL_MAX_DIST = 128
PAGE_SIZE = 128
EPS = 1e-6

LANES = 128
MLA_QK_PAD = 2 * LANES
VMEM_LIMIT = 56 * 1024 * 1024
NEG_INF = float("-inf")


def _params(sem):
    return pltpu.CompilerParams(dimension_semantics=sem, vmem_limit_bytes=VMEM_LIMIT)


def _rms(xf, g):
    return xf * lax.rsqrt(jnp.mean(xf * xf, axis=-1, keepdims=True) + EPS) * g


def _dot(a, b):
    return jnp.dot(a, b, preferred_element_type=F32)


def _dot_nt(a, b):
    return lax.dot_general(a, b, (((1,), (1,)), ((), ())), preferred_element_type=F32)


def _row_tile(m, want):
    t = min(m, want)
    assert m % t == 0
    return t


def _full(shape):
    return pl.BlockSpec(shape, lambda *_: (0,) * len(shape))


def _mla_in_kernel(x_ref, g_ref, w_ref, gq_ref, gkv_ref, cos_ref, sin_ref,
                   cq_ref, ckv_ref, ckvb_ref, kr_ref, krp_ref, *, q_lora, kv_lora):
    h = _rms(x_ref[...], g_ref[...]).astype(BF16)
    z = _dot(h, w_ref[...])
    cq_ref[...] = _rms(z[:, :q_lora], gq_ref[...]).astype(BF16)
    ckv = _rms(z[:, q_lora:q_lora + kv_lora], gkv_ref[...])
    ckv_ref[...] = ckv
    ckvb_ref[...] = ckv.astype(BF16)
    o = q_lora + kv_lora
    kr = z[:, o:o + LANES] * cos_ref[...] + z[:, o + LANES:o + 2 * LANES] * sin_ref[...]
    kr_ref[...] = kr[:, :ROPE_DIM]
    krp_ref[...] = kr.astype(BF16)


def _mla_in(x, g, w_ext, gq, gkv, cos, sin, tm):
    m, d = x.shape
    q_lora, kv_lora = gq.shape[1], gkv.shape[1]
    nrep = cos.shape[0] // tm
    row = lambda i: (i, 0)
    tab = lambda i: (i % nrep, 0)
    return pl.pallas_call(
        functools.partial(_mla_in_kernel, q_lora=q_lora, kv_lora=kv_lora),
        grid=(m // tm,),
        in_specs=[pl.BlockSpec((tm, d), row), _full(g.shape), _full(w_ext.shape),
                  _full(gq.shape), _full(gkv.shape),
                  pl.BlockSpec((tm, LANES), tab), pl.BlockSpec((tm, LANES), tab)],
        out_specs=[pl.BlockSpec((tm, q_lora), row), pl.BlockSpec((tm, kv_lora), row),
                   pl.BlockSpec((tm, kv_lora), row), pl.BlockSpec((tm, ROPE_DIM), row),
                   pl.BlockSpec((tm, LANES), row)],
        out_shape=[jax.ShapeDtypeStruct((m, q_lora), BF16), jax.ShapeDtypeStruct((m, kv_lora), F32),
                   jax.ShapeDtypeStruct((m, kv_lora), BF16), jax.ShapeDtypeStruct((m, ROPE_DIM), F32),
                   jax.ShapeDtypeStruct((m, LANES), BF16)],
        compiler_params=_params(("parallel",)),
        name="mla_in",
    )(x, g, w_ext, gq, gkv, cos, sin)


def _mla_q_kernel(cq_ref, wq_ref, wrot_ref, cos_ref, sin_ref, *rest, absorb):
    if absorb:
        wuk_ref, q_ref, qlat_ref = rest
    else:
        (q_ref,) = rest
    cq = cq_ref[...]
    cos = cos_ref[...]
    sin = sin_ref[...]
    kv_lora = wuk_ref.shape[2] if absorb else 0
    for h in range(MLA_HEADS):
        a = _dot(cq, wq_ref[:, h * MLA_QK_PAD:(h + 1) * MLA_QK_PAD])
        r = _dot(cq, wrot_ref[:, h * LANES:(h + 1) * LANES])
        qn = a[:, :NOPE_DIM].astype(BF16)
        qr = (a[:, NOPE_DIM:] * cos + r * sin).astype(BF16)
        q_ref[:, h * MLA_QK_PAD:h * MLA_QK_PAD + NOPE_DIM] = qn
        q_ref[:, h * MLA_QK_PAD + NOPE_DIM:(h + 1) * MLA_QK_PAD] = qr
        if absorb:
            qlat_ref[:, h * kv_lora:(h + 1) * kv_lora] = _dot(qn, wuk_ref[h]).astype(BF16)


def _mla_q(cq, wq_pad, wrot, cos, sin, tm, wuk_t=None):
    m, q_lora = cq.shape
    nrep = cos.shape[0] // tm
    row = lambda i: (i, 0)
    tab = lambda i: (i % nrep, 0)
    absorb = wuk_t is not None
    in_specs = [pl.BlockSpec((tm, q_lora), row), _full(wq_pad.shape), _full(wrot.shape),
                pl.BlockSpec((tm, LANES), tab), pl.BlockSpec((tm, LANES), tab)]
    out_specs = [pl.BlockSpec((tm, MLA_HEADS * MLA_QK_PAD), row)]
    out_shape = [jax.ShapeDtypeStruct((m, MLA_HEADS * MLA_QK_PAD), BF16)]
    args = [cq, wq_pad, wrot, cos, sin]
    if absorb:
        kv_lora = wuk_t.shape[2]
        in_specs.append(_full(wuk_t.shape))
        out_specs.append(pl.BlockSpec((tm, MLA_HEADS * kv_lora), row))
        out_shape.append(jax.ShapeDtypeStruct((m, MLA_HEADS * kv_lora), BF16))
        args.append(wuk_t)
    return pl.pallas_call(
        functools.partial(_mla_q_kernel, absorb=absorb),
        grid=(m // tm,), in_specs=in_specs, out_specs=out_specs, out_shape=out_shape,
        compiler_params=_params(("parallel",)),
        name="mla_q_absorb" if absorb else "mla_q",
    )(*args)


def _matmul_kernel(a_ref, w_ref, o_ref):
    o_ref[...] = _dot(a_ref[...], w_ref[...]).astype(o_ref.dtype)


def _matmul(a, w, tm, out_dtype):
    m, k = a.shape
    n = w.shape[1]
    return pl.pallas_call(
        _matmul_kernel, grid=(m // tm,),
        in_specs=[pl.BlockSpec((tm, k), lambda i: (i, 0)), _full(w.shape)],
        out_specs=pl.BlockSpec((tm, n), lambda i: (i, 0)),
        out_shape=jax.ShapeDtypeStruct((m, n), out_dtype),
        compiler_params=_params(("parallel",)),
        name="matmul",
    )(a, w)


def _matmul_res_kernel(x_ref, a_ref, w_ref, o_ref):
    o_ref[...] = x_ref[...] + _dot(a_ref[...], w_ref[...])


def _matmul_res(x, a, w, tm):
    m, k = a.shape
    n = w.shape[1]
    return pl.pallas_call(
        _matmul_res_kernel, grid=(m // tm,),
        in_specs=[pl.BlockSpec((tm, n), lambda i: (i, 0)), pl.BlockSpec((tm, k), lambda i: (i, 0)),
                  _full(w.shape)],
        out_specs=pl.BlockSpec((tm, n), lambda i: (i, 0)),
        out_shape=jax.ShapeDtypeStruct((m, n), F32),
        compiler_params=_params(("parallel",)),
        name="matmul_res",
    )(x, a, w)


def _head_matmul_kernel(a_ref, w_ref, o_ref):
    o_ref[...] = _dot(a_ref[0], w_ref[0]).astype(o_ref.dtype)


def _head_matmul(a, w):
    hh, m, k = a.shape
    n = w.shape[2]
    return pl.pallas_call(
        _head_matmul_kernel, grid=(hh,),
        in_specs=[pl.BlockSpec((1, m, k), lambda h: (h, 0, 0)), pl.BlockSpec((1, k, n), lambda h: (h, 0, 0))],
        out_specs=pl.BlockSpec((m, n), lambda h: (0, h)),
        out_shape=jax.ShapeDtypeStruct((m, hh * n), BF16),
        compiler_params=_params(("parallel",)),
        name="head_matmul",
    )(a, w)


def _mla_flash_kernel(q_ref, kn_ref, kr_ref, v_ref, o_ref, *, tq, scale):
    i = pl.program_id(2)
    q = q_ref[...]

    def tile(j, carry, masked):
        m, l, acc = carry
        off = pl.multiple_of(j * tq, tq)
        k = jnp.concatenate([kn_ref[pl.ds(off, tq), :], kr_ref[pl.ds(off, tq), :]], axis=1)
        s = _dot_nt(q, k) * scale
        if masked:
            r = lax.broadcasted_iota(jnp.int32, (tq, tq), 0)
            c = lax.broadcasted_iota(jnp.int32, (tq, tq), 1)
            s = jnp.where(c <= r, s, NEG_INF)
        m_new = jnp.maximum(m, jnp.max(s, axis=-1, keepdims=True))
        alpha = jnp.exp(m - m_new)
        p = jnp.exp(s - m_new)
        l = alpha * l + jnp.sum(p, axis=-1, keepdims=True)
        acc = alpha * acc + _dot(p.astype(BF16), v_ref[pl.ds(off, tq), :])
        return m_new, l, acc

    init = (jnp.full((tq, 1), NEG_INF, F32), jnp.zeros((tq, 1), F32), jnp.zeros((tq, V_DIM), F32))
    carry = tile(i, init, True)
    m, l, acc = lax.fori_loop(0, i, lambda j, c: tile(j, c, False), carry)
    o_ref[...] = (acc / l).astype(o_ref.dtype)


def _mla_flash(q, kv, krp, batch, seq, tq):
    m = q.shape[0]
    nq = seq // tq
    scale = (NOPE_DIM + ROPE_DIM) ** -0.5
    return pl.pallas_call(
        functools.partial(_mla_flash_kernel, tq=tq, scale=scale),
        grid=(batch, MLA_HEADS, nq),
        in_specs=[pl.BlockSpec((tq, MLA_QK_PAD), lambda b, h, i: (b * nq + i, h)),
                  pl.BlockSpec((seq, NOPE_DIM), lambda b, h, i: (b, 2 * h)),
                  pl.BlockSpec((seq, LANES), lambda b, h, i: (b, 0)),
                  pl.BlockSpec((seq, V_DIM), lambda b, h, i: (b, 2 * h + 1))],
        out_specs=pl.BlockSpec((tq, V_DIM), lambda b, h, i: (b * nq + i, h)),
        out_shape=jax.ShapeDtypeStruct((m, MLA_HEADS * V_DIM), BF16),
        compiler_params=_params(("parallel", "parallel", "arbitrary")),
        name="mla_flash",
    )(q, kv, krp, kv)


def _mla_decode_kernel(pt_ref, ql_ref, qr_ref, cn_ref, rn_ref, *rest, n_pg, t_new, scale):
    lat_refs = rest[:n_pg]
    rope_refs = rest[n_pg:2 * n_pg]
    o_ref, m_ref, l_ref, acc_ref = rest[2 * n_pg:]
    j = pl.program_id(1)
    rows = ql_ref.shape[1]

    @pl.when(j == 0)
    def _():
        m_ref[...] = jnp.full(m_ref.shape, NEG_INF, F32)
        l_ref[...] = jnp.zeros(l_ref.shape, F32)
        acc_ref[...] = jnp.zeros(acc_ref.shape, F32)

    ql = ql_ref[0]
    qr = qr_ref[0]
    cs = [lat_refs[g][0, 0].astype(BF16) for g in range(n_pg)]
    s = jnp.concatenate(
        [_dot_nt(ql, cs[g]) + _dot_nt(qr, rope_refs[g][0, 0].astype(BF16)) for g in range(n_pg)],
        axis=1) * scale
    m_old = m_ref[...]
    m_new = jnp.maximum(m_old, jnp.max(s, axis=-1, keepdims=True))
    alpha = jnp.exp(m_old - m_new)
    p = jnp.exp(s - m_new)
    l_ref[...] = alpha * l_ref[...] + jnp.sum(p, axis=-1, keepdims=True)
    pb = p.astype(BF16)
    pv = _dot(pb[:, :PAGE_SIZE], cs[0])
    for g in range(1, n_pg):
        pv = pv + _dot(pb[:, g * PAGE_SIZE:(g + 1) * PAGE_SIZE], cs[g])
    acc_ref[...] = alpha * acc_ref[...] + pv
    m_ref[...] = m_new

    @pl.when(j == pl.num_programs(1) - 1)
    def _():
        cn = cn_ref[0]
        sn = (_dot_nt(ql, cn) + _dot_nt(qr, rn_ref[0])) * scale
        t = lax.broadcasted_iota(jnp.int32, (rows, t_new), 0) % t_new
        u = lax.broadcasted_iota(jnp.int32, (rows, t_new), 1)
        sn = jnp.where(u <= t, sn, NEG_INF)
        m_o = m_ref[...]
        m_n = jnp.maximum(m_o, jnp.max(sn, axis=-1, keepdims=True))
        al = jnp.exp(m_o - m_n)
        pn = jnp.exp(sn - m_n)
        l = al * l_ref[...] + jnp.sum(pn, axis=-1, keepdims=True)
        acc = al * acc_ref[...] + _dot(pn.astype(BF16), cn)
        o_ref[0] = (acc / l).astype(o_ref.dtype)


def _mla_decode(page_table, qlat, qr, cnew, rnew, lat_pool, rope_pool, layer, n_pg):
    bd, rows, c = qlat.shape
    r = qr.shape[2]
    t_new = cnew.shape[1]
    n_pages = page_table.shape[1]
    assert n_pages % n_pg == 0
    scale = (NOPE_DIM + ROPE_DIM) ** -0.5
    per_b = lambda b, j, pt: (b, 0, 0)

    def pool_map(g):
        return lambda b, j, pt: (layer, pt[b, j * n_pg + g], 0, 0)

    in_specs = [pl.BlockSpec((1, rows, c), per_b), pl.BlockSpec((1, rows, r), per_b),
                pl.BlockSpec((1, t_new, c), per_b), pl.BlockSpec((1, t_new, r), per_b)]
    in_specs += [pl.BlockSpec((1, 1, PAGE_SIZE, c), pool_map(g)) for g in range(n_pg)]
    in_specs += [pl.BlockSpec((1, 1, PAGE_SIZE, r), pool_map(g)) for g in range(n_pg)]
    grid_spec = pltpu.PrefetchScalarGridSpec(
        num_scalar_prefetch=1, grid=(bd, n_pages // n_pg), in_specs=in_specs,
        out_specs=pl.BlockSpec((1, rows, c), per_b),
        scratch_shapes=[pltpu.VMEM((rows, 1), F32), pltpu.VMEM((rows, 1), F32), pltpu.VMEM((rows, c), F32)])
    return pl.pallas_call(
        functools.partial(_mla_decode_kernel, n_pg=n_pg, t_new=t_new, scale=scale),
        grid_spec=grid_spec,
        out_shape=jax.ShapeDtypeStruct((bd, rows, c), BF16),
        compiler_params=_params(("parallel", "arbitrary")),
        name="mla_decode",
    )(page_table, qlat, qr, cnew, rnew, *([lat_pool] * n_pg), *([rope_pool] * n_pg))


def _t5_bucket_np(dist):
    n = np.maximum(dist, 0)
    exact = REL_BUCKETS // 2
    nf = np.maximum(n, 1).astype(np.float32)
    large = exact + (np.log(nf / np.float32(exact)) / np.float32(math.log(REL_MAX_DIST / exact))
                     * np.float32(REL_BUCKETS - exact)).astype(np.int32)
    large = np.minimum(large, REL_BUCKETS - 1)
    return np.where(n < exact, n, large).astype(np.int32)


def _bias_kernel(tab_ref, bucket_ref, mask_ref, o_ref):
    h = pl.program_id(0)
    bucket = bucket_ref[...]
    acc = mask_ref[...]
    for b in range(REL_BUCKETS):
        acc = acc + jnp.where(bucket == b, tab_ref[b, h], 0.0)
    o_ref[0] = acc


def _rel_bias(rel_table, bucket, mask_add):
    rr, cc = bucket.shape
    return pl.pallas_call(
        _bias_kernel, grid=(SWA_HEADS,),
        in_specs=[pl.BlockSpec(memory_space=pltpu.SMEM), _full((rr, cc)), _full((rr, cc))],
        out_specs=pl.BlockSpec((1, rr, cc), lambda h: (h, 0, 0)),
        out_shape=jax.ShapeDtypeStruct((SWA_HEADS, rr, cc), F32),
        compiler_params=_params(("parallel",)),
        name="rel_bias",
    )(rel_table, bucket, mask_add)


def _sink_softmax(s, sink):
    m = jnp.maximum(jnp.max(s, axis=-1, keepdims=True), sink)
    e = jnp.exp(s - m)
    return e / (jnp.sum(e, axis=-1, keepdims=True) + jnp.exp(sink - m))


def _rms_matmul_bias_kernel(x_ref, g_ref, w_ref, b_ref, q_ref, k_ref, v_ref, *, nq, nk):
    h = _rms(x_ref[...], g_ref[...]).astype(BF16)
    z = _dot(h, w_ref[...]) + b_ref[...]
    q_ref[...] = z[:, :nq].astype(BF16)
    k_ref[...] = z[:, nq:nq + nk]
    v_ref[...] = z[:, nq + nk:]


def _swa_qkv(x, g, w, b, tm):
    m, d = x.shape
    nq = SWA_HEADS * SWA_HEAD_DIM
    nk = SWA_KV_HEADS * SWA_HEAD_DIM
    row = lambda i: (i, 0)
    return pl.pallas_call(
        functools.partial(_rms_matmul_bias_kernel, nq=nq, nk=nk),
        grid=(m // tm,),
        in_specs=[pl.BlockSpec((tm, d), row), _full(g.shape), _full(w.shape), _full(b.shape)],
        out_specs=[pl.BlockSpec((tm, nq), row), pl.BlockSpec((tm, nk), row), pl.BlockSpec((tm, nk), row)],
        out_shape=[jax.ShapeDtypeStruct((m, nq), BF16), jax.ShapeDtypeStruct((m, nk), F32),
                   jax.ShapeDtypeStruct((m, nk), F32)],
        compiler_params=_params(("parallel",)),
        name="swa_qkv",
    )(x, g, w, b)


def _swa_prompt_kernel(sink_ref, q_ref, kp_ref, kc_ref, vp_ref, vc_ref, bias_ref, o_ref, *, scale):
    q = q_ref[...]
    kk = jnp.concatenate([kp_ref[...], kc_ref[...]], axis=0).astype(BF16)
    vv = jnp.concatenate([vp_ref[...], vc_ref[...]], axis=0).astype(BF16)
    hd = SWA_HEAD_DIM
    for h in range(SWA_HEADS):
        kvh = h // SWA_GROUP
        s = _dot_nt(q[:, h * hd:(h + 1) * hd], kk[:, kvh * hd:(kvh + 1) * hd]) * scale + bias_ref[0, h]
        p = _sink_softmax(s, sink_ref[h]).astype(BF16)
        o_ref[:, h * hd:(h + 1) * hd] = _dot(p, vv[:, kvh * hd:(kvh + 1) * hd]).astype(o_ref.dtype)


def _swa_prompt(q, k, v, bias2, sinks, batch, seq):
    m = q.shape[0]
    nb = seq // WINDOW
    nq = q.shape[1]
    nk = k.shape[1]
    scale = SWA_HEAD_DIM ** -0.5
    cur = lambda b, n: (b * nb + n, 0)
    prev = lambda b, n: (b * nb + jnp.maximum(n - 1, 0), 0)
    return pl.pallas_call(
        functools.partial(_swa_prompt_kernel, scale=scale),
        grid=(batch, nb),
        in_specs=[pl.BlockSpec(memory_space=pltpu.SMEM),
                  pl.BlockSpec((WINDOW, nq), cur),
                  pl.BlockSpec((WINDOW, nk), prev), pl.BlockSpec((WINDOW, nk), cur),
                  pl.BlockSpec((WINDOW, nk), prev), pl.BlockSpec((WINDOW, nk), cur),
                  pl.BlockSpec((1, SWA_HEADS, WINDOW, 2 * WINDOW), lambda b, n: (jnp.minimum(n, 1), 0, 0, 0))],
        out_specs=pl.BlockSpec((WINDOW, nq), cur),
        out_shape=jax.ShapeDtypeStruct((m, nq), BF16),
        compiler_params=_params(("parallel", "arbitrary")),
        name="swa_prompt",
    )(sinks, q, k, k, v, v, bias2)


def _swa_sample_kernel(sink_ref, q_ref, kb_ref, kn_ref, vb_ref, vn_ref, bias_ref,
                       o_ref, ko_ref, vo_ref, *, bblk, t_new, nbuf, kpad, scale):
    hd = SWA_HEAD_DIM
    npad = kpad - nbuf - t_new

    def one(bi, carry):
        row0 = pl.multiple_of(bi * t_new, t_new)
        q = q_ref[pl.ds(row0, t_new), :]
        kn = kn_ref[pl.ds(row0, t_new), :]
        vn = vn_ref[pl.ds(row0, t_new), :]
        kb = kb_ref[bi]
        vb = vb_ref[bi]
        zpad = jnp.zeros((npad, kn.shape[1]), F32)
        kk = jnp.concatenate([kb, kn, zpad], axis=0).astype(BF16)
        vv = jnp.concatenate([vb, vn, zpad], axis=0).astype(BF16)
        ko_ref[bi] = jnp.concatenate([kb[t_new:], kn], axis=0)
        vo_ref[bi] = jnp.concatenate([vb[t_new:], vn], axis=0)
        outs = []
        for kvh in range(SWA_KV_HEADS):
            qs = jnp.concatenate(
                [q[:, (kvh * SWA_GROUP + g) * hd:(kvh * SWA_GROUP + g + 1) * hd] for g in range(SWA_GROUP)],
                axis=0)
            s = _dot_nt(qs, kk[:, kvh * hd:(kvh + 1) * hd]) * scale + bias_ref[kvh]
            sink = jnp.concatenate(
                [jnp.full((t_new, 1), sink_ref[kvh * SWA_GROUP + g], F32) for g in range(SWA_GROUP)], axis=0)
            p = _sink_softmax(s, sink).astype(BF16)
            o = _dot(p, vv[:, kvh * hd:(kvh + 1) * hd])
            outs += [o[g * t_new:(g + 1) * t_new] for g in range(SWA_GROUP)]
        o_ref[pl.ds(row0, t_new), :] = jnp.concatenate(outs, axis=1).astype(o_ref.dtype)
        return carry

    lax.fori_loop(0, bblk, one, 0)


def _swa_sample(q, k, v, buf_k, buf_v, bias, sinks, t_new, bblk):
    bd, nbuf, nk = buf_k.shape
    nq = q.shape[1]
    kpad = bias.shape[2]
    scale = SWA_HEAD_DIM ** -0.5
    rows = lambda i: (i, 0)
    bufs = lambda i: (i, 0, 0)
    return pl.pallas_call(
        functools.partial(_swa_sample_kernel, bblk=bblk, t_new=t_new, nbuf=nbuf, kpad=kpad, scale=scale),
        grid=(bd // bblk,),
        in_specs=[pl.BlockSpec(memory_space=pltpu.SMEM),
                  pl.BlockSpec((bblk * t_new, nq), rows),
                  pl.BlockSpec((bblk, nbuf, nk), bufs), pl.BlockSpec((bblk * t_new, nk), rows),
                  pl.BlockSpec((bblk, nbuf, nk), bufs), pl.BlockSpec((bblk * t_new, nk), rows),
                  _full(bias.shape)],
        out_specs=[pl.BlockSpec((bblk * t_new, nq), rows),
                   pl.BlockSpec((bblk, nbuf, nk), bufs), pl.BlockSpec((bblk, nbuf, nk), bufs)],
        out_shape=[jax.ShapeDtypeStruct((bd * t_new, nq), BF16),
                   jax.ShapeDtypeStruct((bd, nbuf, nk), F32), jax.ShapeDtypeStruct((bd, nbuf, nk), F32)],
        compiler_params=_params(("parallel",)),
        name="swa_sample",
    )(sinks, q, buf_k, k, buf_v, v, bias)


def _mlp_kernel(x_ref, g_ref, w1_ref, w2_ref, o_ref, h_ref):
    f = pl.program_id(1)

    @pl.when(f == 0)
    def _():
        x = x_ref[...]
        h_ref[...] = _rms(x, g_ref[...]).astype(BF16)
        o_ref[...] = x

    a = jnp.maximum(_dot(h_ref[...], w1_ref[...]), 0.0)
    o_ref[...] += _dot((a * a).astype(BF16), w2_ref[...])


def _mlp(x, g, w1, w2, tm, tf):
    m, d = x.shape
    ff = w1.shape[1]
    return pl.pallas_call(
        _mlp_kernel, grid=(m // tm, ff // tf),
        in_specs=[pl.BlockSpec((tm, d), lambda i, f: (i, 0)), pl.BlockSpec((1, d), lambda i, f: (0, 0)),
                  pl.BlockSpec((d, tf), lambda i, f: (0, f)), pl.BlockSpec((tf, d), lambda i, f: (f, 0))],
        out_specs=pl.BlockSpec((tm, d), lambda i, f: (i, 0)),
        out_shape=jax.ShapeDtypeStruct((m, d), F32),
        scratch_shapes=[pltpu.VMEM((tm, d), BF16)],
        compiler_params=_params(("parallel", "arbitrary")),
        name="mlp",
    )(x, g, w1, w2)


def _ple_kernel(x_ref, g_ref, wg_ref, p_ref, wp_ref, *rest, final):
    if final:
        gf_ref, o_ref = rest
    else:
        (o_ref,) = rest
    x = x_ref[...]
    gate = jax.nn.sigmoid(_dot(_rms(x, g_ref[...]).astype(BF16), wg_ref[...]))
    y = x + gate * _dot(p_ref[...].astype(BF16), wp_ref[...])
    if final:
        y = _rms(y, gf_ref[...])
    o_ref[...] = y


def _ple(x, g, w_gate, p, w_proj, tm, g_final=None):
    m, d = x.shape
    pd = p.shape[1]
    row = lambda i: (i, 0)
    final = g_final is not None
    in_specs = [pl.BlockSpec((tm, d), row), _full(g.shape), _full(w_gate.shape),
                pl.BlockSpec((tm, pd), row), _full(w_proj.shape)]
    args = [x, g, w_gate, p, w_proj]
    if final:
        in_specs.append(_full(g_final.shape))
        args.append(g_final)
    return pl.pallas_call(
        functools.partial(_ple_kernel, final=final), grid=(m // tm,),
        in_specs=in_specs, out_specs=pl.BlockSpec((tm, d), row),
        out_shape=jax.ShapeDtypeStruct((m, d), F32),
        compiler_params=_params(("parallel",)),
        name="ple_final" if final else "ple",
    )(*args)


def _rope_tables(pos):
    half = ROPE_DIM // 2
    inv = ROPE_BASE ** (-jnp.arange(half, dtype=F32) / half)
    ang = pos.astype(F32)[:, None] * inv[None, :]
    reps = LANES // half
    return jnp.tile(jnp.cos(ang), (1, reps)), jnp.tile(jnp.sin(ang), (1, reps))


def _rotate_half_cols(w):
    half = ROPE_DIM // 2
    return jnp.concatenate([-w[..., half:], w[..., :half]], axis=-1)


def _prep_mla_weights(w_in, w_qb, w_kvb, q_lora, kv_lora):
    d = w_in.shape[0]
    o = q_lora + kv_lora
    w_r = w_in[:, o:o + ROPE_DIM]
    zpad = jnp.zeros((d, LANES - ROPE_DIM), w_in.dtype)
    w_ext = jnp.concatenate([w_in[:, :o], w_r, zpad, _rotate_half_cols(w_r), zpad], axis=1).astype(BF16)
    wq = w_qb.reshape(q_lora, MLA_HEADS, NOPE_DIM + ROPE_DIM)
    zq = jnp.zeros((q_lora, MLA_HEADS, MLA_QK_PAD - NOPE_DIM - ROPE_DIM), w_qb.dtype)
    wq_pad = jnp.concatenate([wq, zq], axis=2).reshape(q_lora, MLA_HEADS * MLA_QK_PAD).astype(BF16)
    wrot = jnp.concatenate([_rotate_half_cols(wq[..., NOPE_DIM:]), zq], axis=2)
    wrot = wrot.reshape(q_lora, MLA_HEADS * LANES).astype(BF16)
    wkv = w_kvb.reshape(kv_lora, MLA_HEADS, NOPE_DIM + V_DIM)
    wuk_t = jnp.transpose(wkv[..., :NOPE_DIM], (1, 2, 0)).astype(BF16)
    wuv = jnp.transpose(wkv[..., NOPE_DIM:], (1, 0, 2)).astype(BF16)
    return w_ext, wq_pad, wrot, wuk_t, wuv


def _swa_bias_tables(rel_bias, t_new, nbuf, kpad):
    w = WINDOW
    qi = np.arange(w)
    kj = np.arange(2 * w)
    dist = (w + qi[:, None]) - kj[None, :]
    ok = (dist >= 0) & (dist < w)
    mask1 = np.where(ok, 0.0, NEG_INF).astype(np.float32)
    mask0 = np.where(ok & (kj[None, :] >= w), 0.0, NEG_INF).astype(np.float32)
    bucket = _t5_bucket_np(dist)
    bias2 = jnp.stack([_rel_bias(rel_bias, jnp.asarray(bucket), jnp.asarray(mask0)),
                       _rel_bias(rel_bias, jnp.asarray(bucket), jnp.asarray(mask1))])
    kj = np.arange(kpad)
    dist_s = (nbuf + np.arange(t_new))[:, None] - kj[None, :]
    ok_s = (dist_s >= 0) & (dist_s < w) & (kj[None, :] < nbuf + t_new)
    bias_s = _rel_bias(rel_bias, jnp.asarray(_t5_bucket_np(dist_s)),
                       jnp.asarray(np.where(ok_s, 0.0, NEG_INF).astype(np.float32)))
    bias_s = bias_s.reshape(SWA_KV_HEADS, SWA_GROUP * t_new, kpad)
    return bias2, bias_s


def kernel(x_prompt, x_sample, cache_mla_latent, cache_mla_rope, cache_swa_k, cache_swa_v, page_table, p_prompt, p_sample, norm_mix, norm_mlp, norm_ple, norm_final, mla_w_in, mla_g_q, mla_g_kv, mla_w_qb, mla_w_kvb, mla_w_o, swa_w_qkv, swa_b_qkv, swa_sinks, swa_w_o, rel_bias, mlp_w1, mlp_w2, ple_w_gate, ple_w_proj):
    batch, seq, d = x_prompt.shape
    bd, t_new, _ = x_sample.shape
    depth = norm_mix.shape[0]
    q_lora = mla_g_q.shape[1]
    kv_lora = mla_g_kv.shape[1]
    past = page_table.shape[1] * PAGE_SIZE
    nbuf = cache_swa_k.shape[2]
    mp, ms = batch * seq, bd * t_new
    tmp = _row_tile(mp, 512)
    tms = _row_tile(ms, 512)

    xp = x_prompt.reshape(mp, d)
    xs = x_sample.reshape(ms, d)
    row2 = lambda a: a.reshape(1, -1)

    cos_p, sin_p = _rope_tables(jnp.arange(seq))
    cos_s, sin_s = _rope_tables(past + jnp.arange(t_new))
    cos_s, sin_s = jnp.tile(cos_s, (tms // t_new, 1)), jnp.tile(sin_s, (tms // t_new, 1))

    kpad = -(-(nbuf + t_new) // 16) * 16
    bias2, bias_s = _swa_bias_tables(rel_bias, t_new, nbuf, kpad)

    outs = {k: [] for k in ("lat_p", "rope_p", "lat_s", "rope_s", "kp", "vp", "ks", "vs")}
    for i in range(depth):
        j = i // 2
        g_mix = row2(norm_mix[i])
        if i % 2 == 0:
            w_ext, wq_pad, wrot, wuk_t, wuv = _prep_mla_weights(mla_w_in[j], mla_w_qb[j], mla_w_kvb[j],
                                                                q_lora, kv_lora)
            gq, gkv = row2(mla_g_q[j]), row2(mla_g_kv[j])
            w_o = mla_w_o[j].astype(BF16)
            cq, ckv, ckvb, kr, krp = _mla_in(xp, g_mix, w_ext, gq, gkv, cos_p, sin_p, tmp)
            q = _mla_q(cq, wq_pad, wrot, cos_p, sin_p, tmp)[0]
            kv = _matmul(ckvb, mla_w_kvb[j].astype(BF16), tmp, BF16)
            o = _mla_flash(q, kv, krp, batch, seq, _row_tile(seq, 512))
            xp = _matmul_res(xp, o, w_o, tmp)
            outs["lat_p"].append(ckv.reshape(batch, seq, kv_lora))
            outs["rope_p"].append(kr.reshape(batch, seq, ROPE_DIM))
            cq, ckv, ckvb, kr, krp = _mla_in(xs, g_mix, w_ext, gq, gkv, cos_s, sin_s, tms)
            q, qlat = _mla_q(cq, wq_pad, wrot, cos_s, sin_s, tms, wuk_t)
            q4 = q.reshape(bd, t_new, MLA_HEADS, MLA_QK_PAD)[..., NOPE_DIM:NOPE_DIM + ROPE_DIM]
            qr_b = jnp.transpose(q4, (0, 2, 1, 3)).reshape(bd, MLA_HEADS * t_new, ROPE_DIM)
            ql_b = jnp.transpose(qlat.reshape(bd, t_new, MLA_HEADS, kv_lora), (0, 2, 1, 3))
            ql_b = ql_b.reshape(bd, MLA_HEADS * t_new, kv_lora)
            o_lat = _mla_decode(page_table, ql_b, qr_b, ckvb.reshape(bd, t_new, kv_lora),
                                krp[:, :ROPE_DIM].reshape(bd, t_new, ROPE_DIM),
                                cache_mla_latent, cache_mla_rope, j, _row_tile(page_table.shape[1], 16))
            o_lat = jnp.transpose(o_lat.reshape(bd, MLA_HEADS, t_new, kv_lora), (1, 0, 2, 3))
            o = _head_matmul(o_lat.reshape(MLA_HEADS, ms, kv_lora), wuv)
            xs = _matmul_res(xs, o, w_o, tms)
            outs["lat_s"].append(ckv.reshape(bd, t_new, kv_lora))
            outs["rope_s"].append(kr.reshape(bd, t_new, ROPE_DIM))
        else:
            w_qkv = swa_w_qkv[j].astype(BF16)
            b_qkv = row2(swa_b_qkv[j])
            w_o = swa_w_o[j].astype(BF16)
            sinks = swa_sinks[j]
            nk = SWA_KV_HEADS * SWA_HEAD_DIM
            q, k, v = _swa_qkv(xp, g_mix, w_qkv, b_qkv, tmp)
            o = _swa_prompt(q, k, v, bias2, sinks, batch, seq)
            xp = _matmul_res(xp, o, w_o, tmp)
            k5 = k.reshape(batch, seq, SWA_KV_HEADS, SWA_HEAD_DIM)
            v5 = v.reshape(batch, seq, SWA_KV_HEADS, SWA_HEAD_DIM)
            outs["kp"].append(k5[:, -WINDOW:])
            outs["vp"].append(v5[:, -WINDOW:])
            q, k, v = _swa_qkv(xs, g_mix, w_qkv, b_qkv, tms)
            o, k_new, v_new = _swa_sample(q, k, v, cache_swa_k[j].reshape(bd, nbuf, nk),
                                          cache_swa_v[j].reshape(bd, nbuf, nk), bias_s, sinks,
                                          t_new, _row_tile(bd, 8))
            xs = _matmul_res(xs, o, w_o, tms)
            outs["ks"].append(k_new.reshape(bd, nbuf, SWA_KV_HEADS, SWA_HEAD_DIM))
            outs["vs"].append(v_new.reshape(bd, nbuf, SWA_KV_HEADS, SWA_HEAD_DIM))
        w1, w2 = mlp_w1[i].astype(BF16), mlp_w2[i].astype(BF16)
        wg, wp = ple_w_gate[i].astype(BF16), ple_w_proj[i].astype(BF16)
        g_mlp, g_ple = row2(norm_mlp[i]), row2(norm_ple[i])
        g_fin = row2(norm_final) if i == depth - 1 else None
        tf = _row_tile(w1.shape[1], 512)
        xp = _mlp(xp, g_mlp, w1, w2, tmp, tf)
        xs = _mlp(xs, g_mlp, w1, w2, tms, tf)
        xp = _ple(xp, g_ple, wg, p_prompt[i].reshape(mp, -1), wp, tmp, g_fin)
        xs = _ple(xs, g_ple, wg, p_sample[i].reshape(ms, -1), wp, tms, g_fin)

    st = lambda k: jnp.stack(outs[k])
    return (xp.reshape(batch, seq, d), xs.reshape(bd, t_new, d),
            st("lat_p"), st("rope_p"), st("lat_s"), st("rope_s"),
            st("kp"), st("vp"), st("ks"), st("vs"))
```

```python
import functools
import math

import numpy as np
import jax
import jax.numpy as jnp
from jax import lax
from jax.experimental import pallas as pl
from jax.experimental.pallas import tpu as pltpu

F32 = jnp.float32
BF16 = jnp.bfloat16

MLA_HEADS = 16
NOPE_DIM = 128
ROPE_DIM = 64
V_DIM = 128
ROPE_BASE = 10000.0
SWA_HEADS = 32
SWA_KV_HEADS = 8
SWA_GROUP = SWA_HEADS // SWA_KV_HEADS
SWA_HEAD_DIM = 64
WINDOW = 128
REL_BUCKETS = 32
REL_MAX_DIST = 128
PAGE_SIZE = 128
EPS = 1e-6

LANES = 128
MLA_QK_PAD = 2 * LANES
VMEM_LIMIT = 56 * 1024 * 1024
NEG_INF = float("-inf")


def _params(sem):
    return pltpu.CompilerParams(dimension_semantics=sem, vmem_limit_bytes=VMEM_LIMIT)


def _rms(xf, g):
    return xf * lax.rsqrt(jnp.mean(xf * xf, axis=-1, keepdims=True) + EPS) * g


def _dot(a, b):
    return jnp.dot(a, b, preferred_element_type=F32)


def _dot_nt(a, b):
    return lax.dot_general(a, b, (((1,), (1,)), ((), ())), preferred_element_type=F32)


def _row_tile(m, want):
    t = min(m, want)
    assert m % t == 0
    return t


def _full(shape):
    return pl.BlockSpec(shape, lambda *_: (0,) * len(shape))


def _mla_in_kernel(x_ref, g_ref, w_ref, gq_ref, gkv_ref, cos_ref, sin_ref,
                   cq_ref, ckv_ref, ckvb_ref, kr_ref, krp_ref, *, q_lora, kv_lora):
    h = _rms(x_ref[...], g_ref[...]).astype(BF16)
    z = _dot(h, w_ref[...])
    cq_ref[...] = _rms(z[:, :q_lora], gq_ref[...]).astype(BF16)
    ckv = _rms(z[:, q_lora:q_lora + kv_lora], gkv_ref[...])
    ckv_ref[...] = ckv
    ckvb_ref[...] = ckv.astype(BF16)
    o = q_lora + kv_lora
    kr = z[:, o:o + LANES] * cos_ref[...] + z[:, o + LANES:o + 2 * LANES] * sin_ref[...]
    kr_ref[...] = kr[:, :ROPE_DIM]
    krp_ref[...] = kr.astype(BF16)


def _mla_in(x, g, w_ext, gq, gkv, cos, sin, tm):
    m, d = x.shape
    q_lora, kv_lora = gq.shape[1], gkv.shape[1]
    nrep = cos.shape[0] // tm
    row = lambda i: (i, 0)
    tab = lambda i: (i % nrep, 0)
    return pl.pallas_call(
        functools.partial(_mla_in_kernel, q_lora=q_lora, kv_lora=kv_lora),
        grid=(m // tm,),
        in_specs=[pl.BlockSpec((tm, d), row), _full(g.shape), _full(w_ext.shape),
                  _full(gq.shape), _full(gkv.shape),
                  pl.BlockSpec((tm, LANES), tab), pl.BlockSpec((tm, LANES), tab)],
        out_specs=[pl.BlockSpec((tm, q_lora), row), pl.BlockSpec((tm, kv_lora), row),
                   pl.BlockSpec((tm, kv_lora), row), pl.BlockSpec((tm, ROPE_DIM), row),
                   pl.BlockSpec((tm, LANES), row)],
        out_shape=[jax.ShapeDtypeStruct((m, q_lora), BF16), jax.ShapeDtypeStruct((m, kv_lora), F32),
                   jax.ShapeDtypeStruct((m, kv_lora), BF16), jax.ShapeDtypeStruct((m, ROPE_DIM), F32),
                   jax.ShapeDtypeStruct((m, LANES), BF16)],
        compiler_params=_params(("parallel",)),
        name="mla_in",
    )(x, g, w_ext, gq, gkv, cos, sin)


def _mla_q_kernel(cq_ref, wq_ref, wrot_ref, cos_ref, sin_ref, *rest, absorb, q_scale):
    if absorb:
        wuk_ref, q_ref, qlat_ref = rest
    else:
        (q_ref,) = rest
    cq = cq_ref[...]
    cos = cos_ref[...] * q_scale
    sin = sin_ref[...] * q_scale
    kv_lora = wuk_ref.shape[2] if absorb else 0
    for h in range(MLA_HEADS):
        a = _dot(cq, wq_ref[:, h * MLA_QK_PAD:(h + 1) * MLA_QK_PAD])
        r = _dot(cq, wrot_ref[:, h * LANES:(h + 1) * LANES])
        qn = (a[:, :NOPE_DIM] * q_scale).astype(BF16)
        qr = (a[:, NOPE_DIM:] * cos + r * sin).astype(BF16)
        q_ref[:, h * MLA_QK_PAD:h * MLA_QK_PAD + NOPE_DIM] = qn
        q_ref[:, h * MLA_QK_PAD + NOPE_DIM:(h + 1) * MLA_QK_PAD] = qr
        if absorb:
            qlat_ref[:, h * kv_lora:(h + 1) * kv_lora] = _dot(qn, wuk_ref[h]).astype(BF16)


def _mla_q(cq, wq_pad, wrot, cos, sin, tm, q_scale, wuk_t=None):
    m, q_lora = cq.shape
    nrep = cos.shape[0] // tm
    row = lambda i: (i, 0)
    tab = lambda i: (i % nrep, 0)
    absorb = wuk_t is not None
    in_specs = [pl.BlockSpec((tm, q_lora), row), _full(wq_pad.shape), _full(wrot.shape),
                pl.BlockSpec((tm, LANES), tab), pl.BlockSpec((tm, LANES), tab)]
    out_specs = [pl.BlockSpec((tm, MLA_HEADS * MLA_QK_PAD), row)]
    out_shape = [jax.ShapeDtypeStruct((m, MLA_HEADS * MLA_QK_PAD), BF16)]
    args = [cq, wq_pad, wrot, cos, sin]
    if absorb:
        kv_lora = wuk_t.shape[2]
        in_specs.append(_full(wuk_t.shape))
        out_specs.append(pl.BlockSpec((tm, MLA_HEADS * kv_lora), row))
        out_shape.append(jax.ShapeDtypeStruct((m, MLA_HEADS * kv_lora), BF16))
        args.append(wuk_t)
    return pl.pallas_call(
        functools.partial(_mla_q_kernel, absorb=absorb, q_scale=q_scale),
        grid=(m // tm,), in_specs=in_specs, out_specs=out_specs, out_shape=out_shape,
        compiler_params=_params(("parallel",)),
        name="mla_q_absorb" if absorb else "mla_q",
    )(*args)


def _matmul_kernel(a_ref, w_ref, o_ref):
    o_ref[...] = _dot(a_ref[...], w_ref[...]).astype(o_ref.dtype)


def _matmul(a, w, tm, out_dtype):
    m, k = a.shape
    n = w.shape[1]
    return pl.pallas_call(
        _matmul_kernel, grid=(m // tm,),
        in_specs=[pl.BlockSpec((tm, k), lambda i: (i, 0)), _full(w.shape)],
        out_specs=pl.BlockSpec((tm, n), lambda i: (i, 0)),
        out_shape=jax.ShapeDtypeStruct((m, n), out_dtype),
        compiler_params=_params(("parallel",)),
        name="matmul",
    )(a, w)


def _matmul_res_kernel(x_ref, a_ref, w_ref, o_ref):
    o_ref[...] = x_ref[...] + _dot(a_ref[...], w_ref[...])


def _matmul_res(x, a, w, tm):
    m, k = a.shape
    n = w.shape[1]
    return pl.pallas_call(
        _matmul_res_kernel, grid=(m // tm,),
        in_specs=[pl.BlockSpec((tm, n), lambda i: (i, 0)), pl.BlockSpec((tm, k), lambda i: (i, 0)),
                  _full(w.shape)],
        out_specs=pl.BlockSpec((tm, n), lambda i: (i, 0)),
        out_shape=jax.ShapeDtypeStruct((m, n), F32),
        compiler_params=_params(("parallel",)),
        name="matmul_res",
    )(x, a, w)


def _head_matmul_kernel(a_ref, w_ref, o_ref):
    o_ref[...] = _dot(a_ref[0], w_ref[0]).astype(o_ref.dtype)


def _head_matmul(a, w):
    hh, m, k = a.shape
    n = w.shape[2]
    return pl.pallas_call(
        _head_matmul_kernel, grid=(hh,),
        in_specs=[pl.BlockSpec((1, m, k), lambda h: (h, 0, 0)), pl.BlockSpec((1, k, n), lambda h: (h, 0, 0))],
        out_specs=pl.BlockSpec((m, n), lambda h: (0, h)),
        out_shape=jax.ShapeDtypeStruct((m, hh * n), BF16),
        compiler_params=_params(("parallel",)),
        name="head_matmul",
    )(a, w)


def _mla_flash_kernel(q_ref, kn_ref, kr_ref, v_ref, o_ref, vt_ref, *, tq, tk):
    i = pl.program_id(2)
    per_q = tq // tk

    @pl.when(i == 0)
    def _():
        for jj in range(vt_ref.shape[0]):
            vt_ref[jj] = v_ref[jj * tk:(jj + 1) * tk, :].astype(F32).T.astype(BF16)

    qt = q_ref[...].astype(F32).T.astype(BF16)

    def keys(j):
        off = pl.multiple_of(j * tk, tk)
        return jnp.concatenate([kn_ref[pl.ds(off, tk), :], kr_ref[pl.ds(off, tk), :]], axis=1)

    def update(j, st, carry):
        m, l, acc = carry
        m_new = jnp.maximum(m, jnp.max(st, axis=0, keepdims=True))
        alpha = jnp.exp2(m - m_new)
        pt = jnp.exp2(st - m_new)
        l = alpha * l + jnp.sum(pt, axis=0, keepdims=True)
        acc = alpha * acc + _dot(vt_ref[j], pt.astype(BF16))
        return m_new, l, acc

    def body(jj, carry):
        sts = [_dot(keys(jj * per_q + u), qt) for u in range(per_q)]
        for u in range(per_q):
            carry = update(jj * per_q + u, sts[u], carry)
        return carry

    init = (jnp.full((1, tq), NEG_INF, F32), jnp.zeros((1, tq), F32), jnp.zeros((V_DIM, tq), F32))
    carry = lax.fori_loop(0, i, body, init)
    for d in range(per_q):
        c0 = d * tk
        key = lax.broadcasted_iota(jnp.int32, (tk, tq - c0), 0)
        qry = lax.broadcasted_iota(jnp.int32, (tk, tq - c0), 1)
        st = jnp.where(key <= qry, _dot(keys(i * per_q + d), qt[:, c0:]), NEG_INF)
        sub = update(i * per_q + d, st, tuple(x[:, c0:] for x in carry))
        carry = tuple(jnp.concatenate([x[:, :c0], y], axis=1) if c0 else y for x, y in zip(carry, sub))
    m, l, acc = carry
    o_ref[...] = (acc / l).T.astype(o_ref.dtype)


def _mla_flash(q, kv, krp, batch, seq, tq, tk):
    m = q.shape[0]
    nq = seq // tq
    assert tq % tk == 0 and tk % LANES == 0
    return pl.pallas_call(
        functools.partial(_mla_flash_kernel, tq=tq, tk=tk),
        grid=(batch, MLA_HEADS, nq),
        in_specs=[pl.BlockSpec((tq, MLA_QK_PAD), lambda b, h, i: (b * nq + i, h)),
                  pl.BlockSpec((seq, NOPE_DIM), lambda b, h, i: (b, 2 * h)),
                  pl.BlockSpec((seq, LANES), lambda b, h, i: (b, 0)),
                  pl.BlockSpec((seq, V_DIM), lambda b, h, i: (b, 2 * h + 1))],
        out_specs=pl.BlockSpec((tq, V_DIM), lambda b, h, i: (b * nq + i, h)),
        out_shape=jax.ShapeDtypeStruct((m, MLA_HEADS * V_DIM), BF16),
        scratch_shapes=[pltpu.VMEM((seq // tk, V_DIM, tk), BF16)],
        compiler_params=_params(("parallel", "parallel", "arbitrary")),
        name="mla_flash",
    )(q, kv, krp, kv)


def _col(row):
    n = row.shape[1]
    return jnp.broadcast_to(row, (n, n)).T


def _scale_rows(x, col):
    n = col.shape[0]
    return jnp.concatenate([x[:, c * n:(c + 1) * n] * col for c in range(x.shape[1] // n)], axis=1)


def _mla_decode_kernel(pt_ref, qt_ref, kn_ref, lat_hbm, ropet_hbm, o_ref,
                       lat_buf, rope_buf, sem, m_ref, l_ref, acc_ref,
                       *, layer, n_pg, n_sub, n_chunk, t_new, scale):
    b = pl.program_id(0)
    nb = pl.num_programs(0)
    c_dim = lat_buf.shape[2]

    def page_copies(bb, ch, slot):
        cps = []
        for g in range(n_pg):
            page = pt_ref[bb, ch * n_pg + g]
            cps.append(pltpu.make_async_copy(lat_hbm.at[layer, page],
                                             lat_buf.at[slot, pl.ds(g * PAGE_SIZE, PAGE_SIZE)], sem.at[0, slot]))
            cps.append(pltpu.make_async_copy(ropet_hbm.at[layer, page], rope_buf.at[slot, g], sem.at[1, slot]))
        return cps

    @pl.when(b == 0)
    def _():
        for cp in page_copies(0, 0, 0):
            cp.start()

    m_ref[...] = jnp.full(m_ref.shape, NEG_INF, F32)
    l_ref[...] = jnp.zeros(l_ref.shape, F32)
    acc_ref[...] = jnp.zeros(acc_ref.shape, F32)
    qt_lat = qt_ref[0, :c_dim, :]
    qt_rope = qt_ref[0, c_dim:, :]

    def attend(kc, st):
        m_old = m_ref[...]
        m_new = jnp.maximum(m_old, jnp.max(st, axis=0, keepdims=True))
        alpha = jnp.exp(m_old - m_new)
        pt = jnp.exp(st - m_new)
        l_ref[...] = alpha * l_ref[...] + jnp.sum(pt, axis=0, keepdims=True)
        m_ref[...] = m_new
        pv = _dot(pt.T.astype(BF16), kc)
        acc_ref[...] = _scale_rows(acc_ref[...], _col(alpha)) + pv

    pg_sub = n_pg // n_sub

    def chunk(ch, carry):
        slot = ch % 2
        last = ch + 1 == n_chunk
        nxt_b = jnp.where(last, b + 1, b)
        nxt_ch = jnp.where(last, 0, ch + 1)

        @pl.when(nxt_b < nb)
        def _():
            for cp in page_copies(nxt_b, nxt_ch, 1 - slot):
                cp.start()

        for cp in page_copies(b, ch, slot):
            cp.wait()
        kcs, sts = [], []
        for u in range(n_sub):
            kc = lat_buf[slot, u * pg_sub * PAGE_SIZE:(u + 1) * pg_sub * PAGE_SIZE, :].astype(BF16)
            kr = jnp.concatenate([rope_buf[slot, u * pg_sub + g].T for g in range(pg_sub)], axis=0).astype(BF16)
            kcs.append(kc)
            sts.append((_dot(kc, qt_lat) + _dot(kr, qt_rope)) * scale)
        for u in range(n_sub):
            attend(kcs[u], sts[u])
        return carry

    lax.fori_loop(0, n_chunk, chunk, 0)

    kn = kn_ref[0]
    sn = (_dot(kn[:, :c_dim], qt_lat) + _dot(kn[:, c_dim:], qt_rope)) * scale
    u = lax.broadcasted_iota(jnp.int32, sn.shape, 0)
    t = lax.broadcasted_iota(jnp.int32, sn.shape, 1) % t_new
    attend(kn[:, :c_dim], jnp.where(u <= t, sn, NEG_INF))
    o_ref[0] = _scale_rows(acc_ref[...], _col(1.0 / l_ref[...])).astype(o_ref.dtype)


def _mla_decode(page_table, qt, knew, lat_pool, ropet_pool, layer, n_pg, n_sub):
    bd, cr, rows = qt.shape
    c_dim = lat_pool.shape[3]
    r_dim = ropet_pool.shape[2]
    assert cr == c_dim + r_dim and knew.shape == (bd, rows, cr) and rows == LANES
    t_new = rows // MLA_HEADS
    n_pages = page_table.shape[1]
    n_chunk = n_pages // n_pg
    assert n_pages == n_chunk * n_pg and n_chunk % 2 == 0 and n_pg % n_sub == 0
    scale = (NOPE_DIM + ROPE_DIM) ** -0.5
    per_b = lambda b, pt: (b, 0, 0)
    grid_spec = pltpu.PrefetchScalarGridSpec(
        num_scalar_prefetch=1, grid=(bd,),
        in_specs=[pl.BlockSpec((1, cr, rows), per_b), pl.BlockSpec((1, rows, cr), per_b),
                  pl.BlockSpec(memory_space=pl.ANY), pl.BlockSpec(memory_space=pl.ANY)],
        out_specs=pl.BlockSpec((1, rows, c_dim), per_b),
        scratch_shapes=[pltpu.VMEM((2, n_pg * PAGE_SIZE, c_dim), F32),
                        pltpu.VMEM((2, n_pg, r_dim, PAGE_SIZE), F32),
                        pltpu.SemaphoreType.DMA((2, 2)),
                        pltpu.VMEM((1, rows), F32), pltpu.VMEM((1, rows), F32), pltpu.VMEM((rows, c_dim), F32)])
    return pl.pallas_call(
        functools.partial(_mla_decode_kernel, layer=layer, n_pg=n_pg, n_sub=n_sub, n_chunk=n_chunk, t_new=t_new,
                          scale=scale),
        grid_spec=grid_spec,
        out_shape=jax.ShapeDtypeStruct((bd, rows, c_dim), BF16),
        compiler_params=_params(("arbitrary",)),
        name="mla_decode",
    )(page_table, qt, knew, lat_pool, ropet_pool)


def _t5_bucket_np(dist):
    n = np.maximum(dist, 0)
    exact = REL_BUCKETS // 2
    nf = np.maximum(n, 1).astype(np.float32)
    large = exact + (np.log(nf / np.float32(exact)) / np.float32(math.log(REL_MAX_DIST / exact))
                     * np.float32(REL_BUCKETS - exact)).astype(np.int32)
    large = np.minimum(large, REL_BUCKETS - 1)
    return np.where(n < exact, n, large).astype(np.int32)


def _bias_kernel(tab_ref, bucket_ref, mask_ref, o_ref):
    h = pl.program_id(0)
    bucket = bucket_ref[...]
    acc = mask_ref[...]
    for b in range(REL_BUCKETS):
        acc = acc + jnp.where(bucket == b, tab_ref[b, h], 0.0)
    o_ref[0] = acc


def _rel_bias(rel_table, bucket, mask_add):
    rr, cc = bucket.shape
    return pl.pallas_call(
        _bias_kernel, grid=(SWA_HEADS,),
        in_specs=[pl.BlockSpec(memory_space=pltpu.SMEM), _full((rr, cc)), _full((rr, cc))],
        out_specs=pl.BlockSpec((1, rr, cc), lambda h: (h, 0, 0)),
        out_shape=jax.ShapeDtypeStruct((SWA_HEADS, rr, cc), F32),
        compiler_params=_params(("parallel",)),
        name="rel_bias",
    )(rel_table, bucket, mask_add)


def _sink_softmax(s, sink):
    m = jnp.maximum(jnp.max(s, axis=-1, keepdims=True), sink)
    e = jnp.exp(s - m)
    return e / (jnp.sum(e, axis=-1, keepdims=True) + jnp.exp(sink - m))


def _rms_matmul_bias_kernel(x_ref, g_ref, w_ref, b_ref, q_ref, k_ref, v_ref, *, nq, nk):
    h = _rms(x_ref[...], g_ref[...]).astype(BF16)
    z = _dot(h, w_ref[...]) + b_ref[...]
    q_ref[...] = z[:, :nq].astype(BF16)
    k_ref[...] = z[:, nq:nq + nk]
    v_ref[...] = z[:, nq + nk:]


def _swa_qkv(x, g, w, b, tm):
    m, d = x.shape
    nq = SWA_HEADS * SWA_HEAD_DIM
    nk = SWA_KV_HEADS * SWA_HEAD_DIM
    row = lambda i: (i, 0)
    return pl.pallas_call(
        functools.partial(_rms_matmul_bias_kernel, nq=nq, nk=nk),
        grid=(m // tm,),
        in_specs=[pl.BlockSpec((tm, d), row), _full(g.shape), _full(w.shape), _full(b.shape)],
        out_specs=[pl.BlockSpec((tm, nq), row), pl.BlockSpec((tm, nk), row), pl.BlockSpec((tm, nk), row)],
        out_shape=[jax.ShapeDtypeStruct((m, nq), BF16), jax.ShapeDtypeStruct((m, nk), F32),
                   jax.ShapeDtypeStruct((m, nk), F32)],
        compiler_params=_params(("parallel",)),
        name="swa_qkv",
    )(x, g, w, b)


def _swa_prompt_kernel(sink_ref, q_ref, kp_ref, kc_ref, vp_ref, vc_ref, bias_ref, o_ref, *, scale):
    q = q_ref[...]
    kk = jnp.concatenate([kp_ref[...], kc_ref[...]], axis=0).astype(BF16)
    vv = jnp.concatenate([vp_ref[...], vc_ref[...]], axis=0).astype(BF16)
    hd = SWA_HEAD_DIM
    heads = range(SWA_HEADS)
    ss = [_dot_nt(q[:, h * hd:(h + 1) * hd], kk[:, (h // SWA_GROUP) * hd:(h // SWA_GROUP + 1) * hd]) for h in heads]
    ps = [_sink_softmax(ss[h] * scale + bias_ref[0, h], sink_ref[h]).astype(BF16) for h in heads]
    os_ = [_dot(ps[h], vv[:, (h // SWA_GROUP) * hd:(h // SWA_GROUP + 1) * hd]) for h in heads]
    o_ref[...] = jnp.concatenate(os_, axis=1).astype(o_ref.dtype)


def _swa_prompt(q, k, v, bias2, sinks, batch, seq):
    m = q.shape[0]
    nb = seq // WINDOW
    nq = q.shape[1]
    nk = k.shape[1]
    scale = SWA_HEAD_DIM ** -0.5
    cur = lambda b, n: (b * nb + n, 0)
    prev = lambda b, n: (b * nb + jnp.maximum(n - 1, 0), 0)
    return pl.pallas_call(
        functools.partial(_swa_prompt_kernel, scale=scale),
        grid=(batch, nb),
        in_specs=[pl.BlockSpec(memory_space=pltpu.SMEM),
                  pl.BlockSpec((WINDOW, nq), cur),
                  pl.BlockSpec((WINDOW, nk), prev), pl.BlockSpec((WINDOW, nk), cur),
                  pl.BlockSpec((WINDOW, nk), prev), pl.BlockSpec((WINDOW, nk), cur),
                  pl.BlockSpec((1, SWA_HEADS, WINDOW, 2 * WINDOW), lambda b, n: (jnp.minimum(n, 1), 0, 0, 0))],
        out_specs=pl.BlockSpec((WINDOW, nq), cur),
        out_shape=jax.ShapeDtypeStruct((m, nq), BF16),
        compiler_params=_params(("parallel", "arbitrary")),
        name="swa_prompt",
    )(sinks, q, k, k, v, v, bias2)


def _swa_sample_kernel(sink_ref, q_ref, kb_ref, kn_ref, vb_ref, vn_ref, bias_ref,
                       o_ref, ko_ref, vo_ref, *, bblk, bunroll, t_new, nbuf, kpad, scale):
    hd = SWA_HEAD_DIM
    npad = kpad - nbuf - t_new

    kvhs = range(SWA_KV_HEADS)
    sinks = [jnp.concatenate([jnp.full((t_new, 1), sink_ref[kvh * SWA_GROUP + g], F32) for g in range(SWA_GROUP)],
                             axis=0) for kvh in kvhs]

    def load(bi):
        row0 = pl.multiple_of(bi * t_new, t_new)
        q = q_ref[pl.ds(row0, t_new), :]
        kn = kn_ref[pl.ds(row0, t_new), :]
        vn = vn_ref[pl.ds(row0, t_new), :]
        kb = kb_ref[bi]
        vb = vb_ref[bi]
        zpad = jnp.zeros((npad, kn.shape[1]), F32)
        kk = jnp.concatenate([kb, kn, zpad], axis=0).astype(BF16)
        vv = jnp.concatenate([vb, vn, zpad], axis=0).astype(BF16)
        ko_ref[bi] = jnp.concatenate([kb[t_new:], kn], axis=0)
        vo_ref[bi] = jnp.concatenate([vb[t_new:], vn], axis=0)
        qs = [jnp.concatenate([q[:, (kvh * SWA_GROUP + g) * hd:(kvh * SWA_GROUP + g + 1) * hd]
                               for g in range(SWA_GROUP)], axis=0) for kvh in kvhs]
        return qs, kk, vv

    def step(it, carry):
        bis = [it * bunroll + u for u in range(bunroll)]
        data = [load(bi) for bi in bis]
        ss = [[_dot_nt(qs[kvh], kk[:, kvh * hd:(kvh + 1) * hd]) for kvh in kvhs] for qs, kk, _ in data]
        ps = [[_sink_softmax(s[kvh] * scale + bias_ref[kvh], sinks[kvh]).astype(BF16) for kvh in kvhs] for s in ss]
        for bi, p, (_, _, vv) in zip(bis, ps, data):
            outs = []
            for kvh in kvhs:
                o = _dot(p[kvh], vv[:, kvh * hd:(kvh + 1) * hd])
                outs += [o[g * t_new:(g + 1) * t_new] for g in range(SWA_GROUP)]
            o_ref[pl.ds(pl.multiple_of(bi * t_new, t_new), t_new), :] = jnp.concatenate(outs, axis=1).astype(o_ref.dtype)
        return carry

    lax.fori_loop(0, bblk // bunroll, step, 0)


def _swa_sample(q, k, v, buf_k, buf_v, bias, sinks, t_new, bblk, bunroll):
    bd, nbuf, nk = buf_k.shape
    assert bd % bblk == 0 and bblk % bunroll == 0
    nq = q.shape[1]
    kpad = bias.shape[2]
    scale = SWA_HEAD_DIM ** -0.5
    rows = lambda i: (i, 0)
    bufs = lambda i: (i, 0, 0)
    return pl.pallas_call(
        functools.partial(_swa_sample_kernel, bblk=bblk, bunroll=bunroll, t_new=t_new, nbuf=nbuf, kpad=kpad,
                          scale=scale),
        grid=(bd // bblk,),
        in_specs=[pl.BlockSpec(memory_space=pltpu.SMEM),
                  pl.BlockSpec((bblk * t_new, nq), rows),
                  pl.BlockSpec((bblk, nbuf, nk), bufs), pl.BlockSpec((bblk * t_new, nk), rows),
                  pl.BlockSpec((bblk, nbuf, nk), bufs), pl.BlockSpec((bblk * t_new, nk), rows),
                  _full(bias.shape)],
        out_specs=[pl.BlockSpec((bblk * t_new, nq), rows),
                   pl.BlockSpec((bblk, nbuf, nk), bufs), pl.BlockSpec((bblk, nbuf, nk), bufs)],
        out_shape=[jax.ShapeDtypeStruct((bd * t_new, nq), BF16),
                   jax.ShapeDtypeStruct((bd, nbuf, nk), F32), jax.ShapeDtypeStruct((bd, nbuf, nk), F32)],
        compiler_params=_params(("parallel",)),
        name="swa_sample",
    )(sinks, q, buf_k, k, buf_v, v, bias)


def _mlp_kernel(x_ref, g_ref, w1_ref, w2_ref, o_ref, h_ref):
    f = pl.program_id(1)

    @pl.when(f == 0)
    def _():
        x = x_ref[...]
        h_ref[...] = _rms(x, g_ref[...]).astype(BF16)
        o_ref[...] = x

    a = jnp.maximum(_dot(h_ref[...], w1_ref[...]), 0.0)
    o_ref[...] += _dot((a * a).astype(BF16), w2_ref[...])


def _mlp(x, g, w1, w2, layer, tm, tf):
    m, d = x.shape
    ff = w1.shape[2]
    return pl.pallas_call(
        _mlp_kernel, grid=(m // tm, ff // tf),
        in_specs=[pl.BlockSpec((tm, d), lambda i, f: (i, 0)), pl.BlockSpec((1, d), lambda i, f: (0, 0)),
                  pl.BlockSpec((None, d, tf), lambda i, f: (layer, 0, f)),
                  pl.BlockSpec((None, tf, d), lambda i, f: (layer, f, 0))],
        out_specs=pl.BlockSpec((tm, d), lambda i, f: (i, 0)),
        out_shape=jax.ShapeDtypeStruct((m, d), F32),
        scratch_shapes=[pltpu.VMEM((tm, d), BF16)],
        compiler_params=_params(("parallel", "arbitrary")),
        name="mlp",
    )(x, g, w1, w2)


def _ple_kernel(x_ref, g_ref, wg_ref, p_ref, wp_ref, *rest, final):
    if final:
        gf_ref, o_ref = rest
    else:
        (o_ref,) = rest
    x = x_ref[...]
    gate = jax.nn.sigmoid(_dot(_rms(x, g_ref[...]).astype(BF16), wg_ref[...]))
    y = x + gate * _dot(p_ref[...].astype(BF16), wp_ref[...])
    if final:
        y = _rms(y, gf_ref[...])
    o_ref[...] = y


def _ple(x, g, w_gate, p, w_proj, tm, g_final=None):
    m, d = x.shape
    pd = p.shape[1]
    row = lambda i: (i, 0)
    final = g_final is not None
    in_specs = [pl.BlockSpec((tm, d), row), _full(g.shape), _full(w_gate.shape),
                pl.BlockSpec((tm, pd), row), _full(w_proj.shape)]
    args = [x, g, w_gate, p, w_proj]
    if final:
        in_specs.append(_full(g_final.shape))
        args.append(g_final)
    return pl.pallas_call(
        functools.partial(_ple_kernel, final=final), grid=(m // tm,),
        in_specs=in_specs, out_specs=pl.BlockSpec((tm, d), row),
        out_shape=jax.ShapeDtypeStruct((m, d), F32),
        compiler_params=_params(("parallel",)),
        name="ple_final" if final else "ple",
    )(*args)


def _rope_tables(pos):
    half = ROPE_DIM // 2
    inv = ROPE_BASE ** (-jnp.arange(half, dtype=F32) / half)
    ang = pos.astype(F32)[:, None] * inv[None, :]
    reps = LANES // half
    return jnp.tile(jnp.cos(ang), (1, reps)), jnp.tile(jnp.sin(ang), (1, reps))


def _rotate_half_cols(w):
    half = ROPE_DIM // 2
    return jnp.concatenate([-w[..., half:], w[..., :half]], axis=-1)


def _prep_mla_weights(w_in, w_qb, w_kvb, q_lora, kv_lora):
    d = w_in.shape[0]
    o = q_lora + kv_lora
    w_r = w_in[:, o:o + ROPE_DIM]
    zpad = jnp.zeros((d, LANES - ROPE_DIM), w_in.dtype)
    w_ext = jnp.concatenate([w_in[:, :o], w_r, zpad, _rotate_half_cols(w_r), zpad], axis=1).astype(BF16)
    wq = w_qb.reshape(q_lora, MLA_HEADS, NOPE_DIM + ROPE_DIM)
    zq = jnp.zeros((q_lora, MLA_HEADS, MLA_QK_PAD - NOPE_DIM - ROPE_DIM), w_qb.dtype)
    wq_pad = jnp.concatenate([wq, zq], axis=2).reshape(q_lora, MLA_HEADS * MLA_QK_PAD).astype(BF16)
    wrot = jnp.concatenate([_rotate_half_cols(wq[..., NOPE_DIM:]), zq], axis=2)
    wrot = wrot.reshape(q_lora, MLA_HEADS * LANES).astype(BF16)
    wkv = w_kvb.reshape(kv_lora, MLA_HEADS, NOPE_DIM + V_DIM)
    wuk_t = jnp.transpose(wkv[..., :NOPE_DIM], (1, 2, 0)).astype(BF16)
    wuv = jnp.transpose(wkv[..., NOPE_DIM:], (1, 0, 2)).astype(BF16)
    return w_ext, wq_pad, wrot, wuk_t, wuv


def _swa_bias_tables(rel_bias, t_new, nbuf, kpad):
    w = WINDOW
    qi = np.arange(w)
    kj = np.arange(2 * w)
    dist = (w + qi[:, None]) - kj[None, :]
    ok = (dist >= 0) & (dist < w)
    mask1 = np.where(ok, 0.0, NEG_INF).astype(np.float32)
    mask0 = np.where(ok & (kj[None, :] >= w), 0.0, NEG_INF).astype(np.float32)
    bucket = _t5_bucket_np(dist)
    bias2 = jnp.stack([_rel_bias(rel_bias, jnp.asarray(bucket), jnp.asarray(mask0)),
                       _rel_bias(rel_bias, jnp.asarray(bucket), jnp.asarray(mask1))])
    kj = np.arange(kpad)
    dist_s = (nbuf + np.arange(t_new))[:, None] - kj[None, :]
    ok_s = (dist_s >= 0) & (dist_s < w) & (kj[None, :] < nbuf + t_new)
    bias_s = _rel_bias(rel_bias, jnp.asarray(_t5_bucket_np(dist_s)),
                       jnp.asarray(np.where(ok_s, 0.0, NEG_INF).astype(np.float32)))
    bias_s = bias_s.reshape(SWA_KV_HEADS, SWA_GROUP * t_new, kpad)
    return bias2, bias_s


def kernel(x_prompt, x_sample, cache_mla_latent, cache_mla_rope, cache_swa_k, cache_swa_v, page_table, p_prompt, p_sample, norm_mix, norm_mlp, norm_ple, norm_final, mla_w_in, mla_g_q, mla_g_kv, mla_w_qb, mla_w_kvb, mla_w_o, swa_w_qkv, swa_b_qkv, swa_sinks, swa_w_o, rel_bias, mlp_w1, mlp_w2, ple_w_gate, ple_w_proj):
    batch, seq, d = x_prompt.shape
    bd, t_new, _ = x_sample.shape
    depth = norm_mix.shape[0]
    q_lora = mla_g_q.shape[1]
    kv_lora = mla_g_kv.shape[1]
    past = page_table.shape[1] * PAGE_SIZE
    nbuf = cache_swa_k.shape[2]
    mp, ms = batch * seq, bd * t_new
    mla_scale = (NOPE_DIM + ROPE_DIM) ** -0.5
    tmp = _row_tile(mp, 512)
    tms = _row_tile(ms, 512)

    xp = x_prompt.reshape(mp, d)
    xs = x_sample.reshape(ms, d)
    row2 = lambda a: a.reshape(1, -1)

    cos_p, sin_p = _rope_tables(jnp.arange(seq))
    cos_s, sin_s = _rope_tables(past + jnp.arange(t_new))
    cos_s, sin_s = jnp.tile(cos_s, (tms // t_new, 1)), jnp.tile(sin_s, (tms // t_new, 1))

    kpad = -(-(nbuf + t_new) // 16) * 16
    bias2, bias_s = _swa_bias_tables(rel_bias, t_new, nbuf, kpad)

    w1_all, w2_all = mlp_w1.astype(BF16), mlp_w2.astype(BF16)
    outs = {k: [] for k in ("lat_p", "rope_p", "lat_s", "rope_s", "kp", "vp", "ks", "vs")}
    for i in range(depth):
        j = i // 2
        g_mix = row2(norm_mix[i])
        if i % 2 == 0:
            w_ext, wq_pad, wrot, wuk_t, wuv = _prep_mla_weights(mla_w_in[j], mla_w_qb[j], mla_w_kvb[j],
                                                                q_lora, kv_lora)
            gq, gkv = row2(mla_g_q[j]), row2(mla_g_kv[j])
            w_o = mla_w_o[j].astype(BF16)
            cq, ckv, ckvb, kr, krp = _mla_in(xp, g_mix, w_ext, gq, gkv, cos_p, sin_p, tmp)
            q = _mla_q(cq, wq_pad, wrot, cos_p, sin_p, tmp, mla_scale * math.log2(math.e))[0]
            kv = _matmul(ckvb, mla_w_kvb[j].astype(BF16), tmp, BF16)
            o = _mla_flash(q, kv, krp, batch, seq, _row_tile(seq, 1024), _row_tile(seq, 512))
            xp = _matmul_res(xp, o, w_o, tmp)
            outs["lat_p"].append(ckv.reshape(batch, seq, kv_lora))
            outs["rope_p"].append(kr.reshape(batch, seq, ROPE_DIM))
            cq, ckv, ckvb, kr, krp = _mla_in(xs, g_mix, w_ext, gq, gkv, cos_s, sin_s, tms)
            q, qlat = _mla_q(cq, wq_pad, wrot, cos_s, sin_s, tms, 1.0, wuk_t)
            q4 = q.reshape(bd, t_new, MLA_HEADS, MLA_QK_PAD)[..., NOPE_DIM:NOPE_DIM + ROPE_DIM]
            qt = jnp.concatenate([jnp.transpose(qlat.reshape(bd, t_new, MLA_HEADS, kv_lora), (0, 3, 2, 1)),
                                  jnp.transpose(q4, (0, 3, 2, 1))], axis=1)
            qt = qt.reshape(bd, kv_lora + ROPE_DIM, MLA_HEADS * t_new)
            knew = jnp.concatenate([ckvb.reshape(bd, t_new, kv_lora),
                                    krp[:, :ROPE_DIM].reshape(bd, t_new, ROPE_DIM)], axis=2)
            knew = jnp.pad(knew, ((0, 0), (0, MLA_HEADS * t_new - t_new), (0, 0)))
            o_lat = _mla_decode(page_table, qt, knew, cache_mla_latent, jnp.swapaxes(cache_mla_rope, 2, 3),
                                j, _row_tile(page_table.shape[1], 32), 8)
            o_lat = jnp.transpose(o_lat.reshape(bd, MLA_HEADS, t_new, kv_lora), (1, 0, 2, 3))
            o = _head_matmul(o_lat.reshape(MLA_HEADS, ms, kv_lora), wuv)
            xs = _matmul_res(xs, o, w_o, tms)
            outs["lat_s"].append(ckv.reshape(bd, t_new, kv_lora))
            outs["rope_s"].append(kr.reshape(bd, t_new, ROPE_DIM))
        else:
            w_qkv = swa_w_qkv[j].astype(BF16)
            b_qkv = row2(swa_b_qkv[j])
            w_o = swa_w_o[j].astype(BF16)
            sinks = swa_sinks[j]
            nk = SWA_KV_HEADS * SWA_HEAD_DIM
            q, k, v = _swa_qkv(xp, g_mix, w_qkv, b_qkv, tmp)
            o = _swa_prompt(q, k, v, bias2, sinks, batch, seq)
            xp = _matmul_res(xp, o, w_o, tmp)
            tail = lambda a: a.reshape(batch, seq, nk)[:, -WINDOW:].reshape(batch, WINDOW, SWA_KV_HEADS, SWA_HEAD_DIM)
            outs["kp"].append(tail(k))
            outs["vp"].append(tail(v))
            q, k, v = _swa_qkv(xs, g_mix, w_qkv, b_qkv, tms)
            o, k_new, v_new = _swa_sample(q, k, v, cache_swa_k[j].reshape(bd, nbuf, nk),
                                          cache_swa_v[j].reshape(bd, nbuf, nk), bias_s, sinks,
                                          t_new, _row_tile(bd, 8), 1)
            xs = _matmul_res(xs, o, w_o, tms)
            outs["ks"].append(k_new.reshape(bd, nbuf, SWA_KV_HEADS, SWA_HEAD_DIM))
            outs["vs"].append(v_new.reshape(bd, nbuf, SWA_KV_HEADS, SWA_HEAD_DIM))
        wg, wp = ple_w_gate[i].astype(BF16), ple_w_proj[i].astype(BF16)
        g_mlp, g_ple = row2(norm_mlp[i]), row2(norm_ple[i])
        g_fin = row2(norm_final) if i == depth - 1 else None
        tf = _row_tile(w1_all.shape[2], 1024)
        xp = _mlp(xp, g_mlp, w1_all, w2_all, i, tmp, tf)
        xs = _mlp(xs, g_mlp, w1_all, w2_all, i, tms, tf)
        xp = _ple(xp, g_ple, wg, p_prompt[i].reshape(mp, -1), wp, tmp, g_fin)
        xs = _ple(xs, g_ple, wg, p_sample[i].reshape(ms, -1), wp, tms, g_fin)

    st = lambda k: jnp.stack(outs[k])
    return (xp.reshape(batch, seq, d), xs.reshape(bd, t_new, d),
            st("lat_p"), st("rope_p"), st("lat_s"), st("rope_s"),
            st("kp"), st("vp"), st("ks"), st("vs"))
```

```python
import functools
import math

import numpy as np
import jax
import jax.numpy as jnp
from jax import lax
from jax.experimental import pallas as pl
from jax.experimental.pallas import tpu as pltpu

F32 = jnp.float32
BF16 = jnp.bfloat16

MLA_HEADS = 16
NOPE_DIM = 128
ROPE_DIM = 64
V_DIM = 128
ROPE_BASE = 10000.0
SWA_HEADS = 32
SWA_KV_HEADS = 8
SWA_GROUP = SWA_HEADS // SWA_KV_HEADS
SWA_HEAD_DIM = 64
WINDOW = 128
REL_BUCKETS = 32
REL_MAX_DIST = 128
PAGE_SIZE = 128
EPS = 1e-6

LANES = 128
MLA_QK_PAD = 2 * LANES
VMEM_LIMIT = 56 * 1024 * 1024
NEG_INF = float("-inf")


def _params(sem):
    return pltpu.CompilerParams(dimension_semantics=sem, vmem_limit_bytes=VMEM_LIMIT)


def _rms(xf, g):
    return xf * lax.rsqrt(jnp.mean(xf * xf, axis=-1, keepdims=True) + EPS) * g


def _dot(a, b):
    return jnp.dot(a, b, preferred_element_type=F32)


def _dot_nt(a, b):
    return lax.dot_general(a, b, (((1,), (1,)), ((), ())), preferred_element_type=F32)


def _row_tile(m, want):
    t = min(m, want)
    assert m % t == 0
    return t


def _full(shape):
    return pl.BlockSpec(shape, lambda *_: (0,) * len(shape))


def _mla_in_kernel(x_ref, g_ref, w_ref, gq_ref, gkv_ref, cos_ref, sin_ref,
                   cq_ref, ckv_ref, ckvb_ref, kr_ref, krp_ref, *, q_lora, kv_lora):
    h = _rms(x_ref[...], g_ref[...]).astype(BF16)
    z = _dot(h, w_ref[...])
    cq_ref[...] = _rms(z[:, :q_lora], gq_ref[...]).astype(BF16)
    ckv = _rms(z[:, q_lora:q_lora + kv_lora], gkv_ref[...])
    ckv_ref[...] = ckv
    ckvb_ref[...] = ckv.astype(BF16)
    o = q_lora + kv_lora
    kr = z[:, o:o + LANES] * cos_ref[...] + z[:, o + LANES:o + 2 * LANES] * sin_ref[...]
    kr_ref[...] = kr[:, :ROPE_DIM]
    krp_ref[...] = kr.astype(BF16)


def _mla_in(x, g, w_ext, gq, gkv, cos, sin, tm):
    m, d = x.shape
    q_lora, kv_lora = gq.shape[1], gkv.shape[1]
    nrep = cos.shape[0] // tm
    row = lambda i: (i, 0)
    tab = lambda i: (i % nrep, 0)
    return pl.pallas_call(
        functools.partial(_mla_in_kernel, q_lora=q_lora, kv_lora=kv_lora),
        grid=(m // tm,),
        in_specs=[pl.BlockSpec((tm, d), row), _full(g.shape), _full(w_ext.shape),
                  _full(gq.shape), _full(gkv.shape),
                  pl.BlockSpec((tm, LANES), tab), pl.BlockSpec((tm, LANES), tab)],
        out_specs=[pl.BlockSpec((tm, q_lora), row), pl.BlockSpec((tm, kv_lora), row),
                   pl.BlockSpec((tm, kv_lora), row), pl.BlockSpec((tm, ROPE_DIM), row),
                   pl.BlockSpec((tm, LANES), row)],
        out_shape=[jax.ShapeDtypeStruct((m, q_lora), BF16), jax.ShapeDtypeStruct((m, kv_lora), F32),
                   jax.ShapeDtypeStruct((m, kv_lora), BF16), jax.ShapeDtypeStruct((m, ROPE_DIM), F32),
                   jax.ShapeDtypeStruct((m, LANES), BF16)],
        compiler_params=_params(("parallel",)),
        name="mla_in",
    )(x, g, w_ext, gq, gkv, cos, sin)


def _mla_q_kernel(cq_ref, wq_ref, wrot_ref, cos_ref, sin_ref, *rest, absorb, q_scale):
    if absorb:
        wuk_ref, q_ref, qlat_ref = rest
    else:
        (q_ref,) = rest
    cq = cq_ref[...]
    cos = cos_ref[...] * q_scale
    sin = sin_ref[...] * q_scale
    kv_lora = wuk_ref.shape[2] if absorb else 0
    for h in range(MLA_HEADS):
        a = _dot(cq, wq_ref[:, h * MLA_QK_PAD:(h + 1) * MLA_QK_PAD])
        r = _dot(cq, wrot_ref[:, h * LANES:(h + 1) * LANES])
        qn = (a[:, :NOPE_DIM] * q_scale).astype(BF16)
        qr = (a[:, NOPE_DIM:] * cos + r * sin).astype(BF16)
        q_ref[:, h * MLA_QK_PAD:h * MLA_QK_PAD + NOPE_DIM] = qn
        q_ref[:, h * MLA_QK_PAD + NOPE_DIM:(h + 1) * MLA_QK_PAD] = qr
        if absorb:
            qlat_ref[:, h * kv_lora:(h + 1) * kv_lora] = _dot(qn, wuk_ref[h]).astype(BF16)


def _mla_q(cq, wq_pad, wrot, cos, sin, tm, q_scale, wuk_t=None):
    m, q_lora = cq.shape
    nrep = cos.shape[0] // tm
    row = lambda i: (i, 0)
    tab = lambda i: (i % nrep, 0)
    absorb = wuk_t is not None
    in_specs = [pl.BlockSpec((tm, q_lora), row), _full(wq_pad.shape), _full(wrot.shape),
                pl.BlockSpec((tm, LANES), tab), pl.BlockSpec((tm, LANES), tab)]
    out_specs = [pl.BlockSpec((tm, MLA_HEADS * MLA_QK_PAD), row)]
    out_shape = [jax.ShapeDtypeStruct((m, MLA_HEADS * MLA_QK_PAD), BF16)]
    args = [cq, wq_pad, wrot, cos, sin]
    if absorb:
        kv_lora = wuk_t.shape[2]
        in_specs.append(_full(wuk_t.shape))
        out_specs.append(pl.BlockSpec((tm, MLA_HEADS * kv_lora), row))
        out_shape.append(jax.ShapeDtypeStruct((m, MLA_HEADS * kv_lora), BF16))
        args.append(wuk_t)
    return pl.pallas_call(
        functools.partial(_mla_q_kernel, absorb=absorb, q_scale=q_scale),
        grid=(m // tm,), in_specs=in_specs, out_specs=out_specs, out_shape=out_shape,
        compiler_params=_params(("parallel",)),
        name="mla_q_absorb" if absorb else "mla_q",
    )(*args)


def _matmul_kernel(a_ref, w_ref, o_ref):
    o_ref[...] = _dot(a_ref[...], w_ref[...]).astype(o_ref.dtype)


def _matmul(a, w, tm, out_dtype):
    m, k = a.shape
    n = w.shape[1]
    return pl.pallas_call(
        _matmul_kernel, grid=(m // tm,),
        in_specs=[pl.BlockSpec((tm, k), lambda i: (i, 0)), _full(w.shape)],
        out_specs=pl.BlockSpec((tm, n), lambda i: (i, 0)),
        out_shape=jax.ShapeDtypeStruct((m, n), out_dtype),
        compiler_params=_params(("parallel",)),
        name="matmul",
    )(a, w)


def _matmul_res_kernel(x_ref, a_ref, w_ref, o_ref):
    o_ref[...] = x_ref[...] + _dot(a_ref[...], w_ref[...])


def _matmul_res(x, a, w, tm):
    m, k = a.shape
    n = w.shape[1]
    return pl.pallas_call(
        _matmul_res_kernel, grid=(m // tm,),
        in_specs=[pl.BlockSpec((tm, n), lambda i: (i, 0)), pl.BlockSpec((tm, k), lambda i: (i, 0)),
                  _full(w.shape)],
        out_specs=pl.BlockSpec((tm, n), lambda i: (i, 0)),
        out_shape=jax.ShapeDtypeStruct((m, n), F32),
        compiler_params=_params(("parallel",)),
        name="matmul_res",
    )(x, a, w)


def _head_matmul_kernel(a_ref, w_ref, o_ref):
    o_ref[...] = _dot(a_ref[0], w_ref[0]).astype(o_ref.dtype)


def _head_matmul(a, w):
    hh, m, k = a.shape
    n = w.shape[2]
    return pl.pallas_call(
        _head_matmul_kernel, grid=(hh,),
        in_specs=[pl.BlockSpec((1, m, k), lambda h: (h, 0, 0)), pl.BlockSpec((1, k, n), lambda h: (h, 0, 0))],
        out_specs=pl.BlockSpec((m, n), lambda h: (0, h)),
        out_shape=jax.ShapeDtypeStruct((m, hh * n), BF16),
        compiler_params=_params(("parallel",)),
        name="head_matmul",
    )(a, w)


def _mla_flash_kernel(q_ref, kn_ref, kr_ref, v_ref, o_ref, vt_ref, *, tq, tk):
    i = pl.program_id(2)
    per_q = tq // tk

    @pl.when(i == 0)
    def _():
        for jj in range(vt_ref.shape[0]):
            vt_ref[jj] = v_ref[jj * tk:(jj + 1) * tk, :].astype(F32).T.astype(BF16)

    qt = q_ref[...].astype(F32).T.astype(BF16)

    def keys(j):
        off = pl.multiple_of(j * tk, tk)
        return jnp.concatenate([kn_ref[pl.ds(off, tk), :], kr_ref[pl.ds(off, tk), :]], axis=1)

    def update(j, st, carry):
        m, l, acc = carry
        m_new = jnp.maximum(m, jnp.max(st, axis=0, keepdims=True))
        alpha = jnp.exp2(m - m_new)
        pt = jnp.exp2(st - m_new)
        l = alpha * l + jnp.sum(pt, axis=0, keepdims=True)
        acc = alpha * acc + _dot(vt_ref[j], pt.astype(BF16))
        return m_new, l, acc

    def body(jj, carry):
        sts = [_dot(keys(jj * per_q + u), qt) for u in range(per_q)]
        for u in range(per_q):
            carry = update(jj * per_q + u, sts[u], carry)
        return carry

    init = (jnp.full((1, tq), NEG_INF, F32), jnp.zeros((1, tq), F32), jnp.zeros((V_DIM, tq), F32))
    carry = lax.fori_loop(0, i, body, init)
    for d in range(per_q):
        c0 = d * tk
        key = lax.broadcasted_iota(jnp.int32, (tk, tq - c0), 0)
        qry = lax.broadcasted_iota(jnp.int32, (tk, tq - c0), 1)
        st = jnp.where(key <= qry, _dot(keys(i * per_q + d), qt[:, c0:]), NEG_INF)
        sub = update(i * per_q + d, st, tuple(x[:, c0:] for x in carry))
        carry = tuple(jnp.concatenate([x[:, :c0], y], axis=1) if c0 else y for x, y in zip(carry, sub))
    m, l, acc = carry
    o_ref[...] = (acc / l).T.astype(o_ref.dtype)


def _mla_flash(q, kv, krp, batch, seq, tq, tk):
    m = q.shape[0]
    nq = seq // tq
    assert tq % tk == 0 and tk % LANES == 0
    return pl.pallas_call(
        functools.partial(_mla_flash_kernel, tq=tq, tk=tk),
        grid=(batch, MLA_HEADS, nq),
        in_specs=[pl.BlockSpec((tq, MLA_QK_PAD), lambda b, h, i: (b * nq + i, h)),
                  pl.BlockSpec((seq, NOPE_DIM), lambda b, h, i: (b, 2 * h)),
                  pl.BlockSpec((seq, LANES), lambda b, h, i: (b, 0)),
                  pl.BlockSpec((seq, V_DIM), lambda b, h, i: (b, 2 * h + 1))],
        out_specs=pl.BlockSpec((tq, V_DIM), lambda b, h, i: (b * nq + i, h)),
        out_shape=jax.ShapeDtypeStruct((m, MLA_HEADS * V_DIM), BF16),
        scratch_shapes=[pltpu.VMEM((seq // tk, V_DIM, tk), BF16)],
        compiler_params=_params(("parallel", "parallel", "arbitrary")),
        name="mla_flash",
    )(q, kv, krp, kv)


def _col(row):
    n = row.shape[1]
    return jnp.broadcast_to(row, (n, n)).T


def _scale_rows(x, col):
    n = col.shape[0]
    return jnp.concatenate([x[:, c * n:(c + 1) * n] * col for c in range(x.shape[1] // n)], axis=1)


def _mla_decode_kernel(pt_ref, qt_ref, kn_ref, lat_hbm, ropet_hbm, o_ref,
                       lat_buf, rope_buf, sem, m_ref, l_ref, acc_ref,
                       *, layer, n_pg, n_sub, n_chunk, t_new, scale):
    b = pl.program_id(0)
    nb = pl.num_programs(0)
    c_dim = lat_buf.shape[2]

    def page_copies(bb, ch, slot):
        cps = []
        for g in range(n_pg):
            page = pt_ref[bb, ch * n_pg + g]
            cps.append(pltpu.make_async_copy(lat_hbm.at[layer, page],
                                             lat_buf.at[slot, pl.ds(g * PAGE_SIZE, PAGE_SIZE)], sem.at[0, slot]))
            cps.append(pltpu.make_async_copy(ropet_hbm.at[layer, page], rope_buf.at[slot, g], sem.at[1, slot]))
        return cps

    @pl.when(b == 0)
    def _():
        for cp in page_copies(0, 0, 0):
            cp.start()

    m_ref[...] = jnp.full(m_ref.shape, NEG_INF, F32)
    l_ref[...] = jnp.zeros(l_ref.shape, F32)
    acc_ref[...] = jnp.zeros(acc_ref.shape, F32)
    qt_lat = qt_ref[0, :c_dim, :]
    qt_rope = qt_ref[0, c_dim:, :]

    def attend(kc, st):
        m_old = m_ref[...]
        m_new = jnp.maximum(m_old, jnp.max(st, axis=0, keepdims=True))
        alpha = jnp.exp(m_old - m_new)
        pt = jnp.exp(st - m_new)
        l_ref[...] = alpha * l_ref[...] + jnp.sum(pt, axis=0, keepdims=True)
        m_ref[...] = m_new
        pv = _dot(pt.T.astype(BF16), kc)
        acc_ref[...] = _scale_rows(acc_ref[...], _col(alpha)) + pv

    pg_sub = n_pg // n_sub

    def chunk(ch, carry):
        slot = ch % 2
        last = ch + 1 == n_chunk
        nxt_b = jnp.where(last, b + 1, b)
        nxt_ch = jnp.where(last, 0, ch + 1)

        @pl.when(nxt_b < nb)
        def _():
            for cp in page_copies(nxt_b, nxt_ch, 1 - slot):
                cp.start()

        for cp in page_copies(b, ch, slot):
            cp.wait()
        kcs, sts = [], []
        for u in range(n_sub):
            kc = lat_buf[slot, u * pg_sub * PAGE_SIZE:(u + 1) * pg_sub * PAGE_SIZE, :].astype(BF16)
            kr = jnp.concatenate([rope_buf[slot, u * pg_sub + g].T for g in range(pg_sub)], axis=0).astype(BF16)
            kcs.append(kc)
            sts.append((_dot(kc, qt_lat) + _dot(kr, qt_rope)) * scale)
        for u in range(n_sub):
            attend(kcs[u], sts[u])
        return carry

    lax.fori_loop(0, n_chunk, chunk, 0)

    kn = kn_ref[0]
    sn = (_dot(kn[:, :c_dim], qt_lat) + _dot(kn[:, c_dim:], qt_rope)) * scale
    u = lax.broadcasted_iota(jnp.int32, sn.shape, 0)
    t = lax.broadcasted_iota(jnp.int32, sn.shape, 1) % t_new
    attend(kn[:, :c_dim], jnp.where(u <= t, sn, NEG_INF))
    o_ref[0] = _scale_rows(acc_ref[...], _col(1.0 / l_ref[...])).astype(o_ref.dtype)


def _mla_decode(page_table, qt, knew, lat_pool, ropet_pool, layer, n_pg, n_sub):
    bd, cr, rows = qt.shape
    c_dim = lat_pool.shape[3]
    r_dim = ropet_pool.shape[2]
    assert cr == c_dim + r_dim and knew.shape == (bd, rows, cr) and rows == LANES
    t_new = rows // MLA_HEADS
    n_pages = page_table.shape[1]
    n_chunk = n_pages // n_pg
    assert n_pages == n_chunk * n_pg and n_chunk % 2 == 0 and n_pg % n_sub == 0
    scale = (NOPE_DIM + ROPE_DIM) ** -0.5
    per_b = lambda b, pt: (b, 0, 0)
    grid_spec = pltpu.PrefetchScalarGridSpec(
        num_scalar_prefetch=1, grid=(bd,),
        in_specs=[pl.BlockSpec((1, cr, rows), per_b), pl.BlockSpec((1, rows, cr), per_b),
                  pl.BlockSpec(memory_space=pl.ANY), pl.BlockSpec(memory_space=pl.ANY)],
        out_specs=pl.BlockSpec((1, rows, c_dim), per_b),
        scratch_shapes=[pltpu.VMEM((2, n_pg * PAGE_SIZE, c_dim), F32),
                        pltpu.VMEM((2, n_pg, r_dim, PAGE_SIZE), F32),
                        pltpu.SemaphoreType.DMA((2, 2)),
                        pltpu.VMEM((1, rows), F32), pltpu.VMEM((1, rows), F32), pltpu.VMEM((rows, c_dim), F32)])
    return pl.pallas_call(
        functools.partial(_mla_decode_kernel, layer=layer, n_pg=n_pg, n_sub=n_sub, n_chunk=n_chunk, t_new=t_new,
                          scale=scale),
        grid_spec=grid_spec,
        out_shape=jax.ShapeDtypeStruct((bd, rows, c_dim), BF16),
        compiler_params=_params(("arbitrary",)),
        name="mla_decode",
    )(page_table, qt, knew, lat_pool, ropet_pool)


def _t5_bucket_np(dist):
    n = np.maximum(dist, 0)
    exact = REL_BUCKETS // 2
    nf = np.maximum(n, 1).astype(np.float32)
    large = exact + (np.log(nf / np.float32(exact)) / np.float32(math.log(REL_MAX_DIST / exact))
                     * np.float32(REL_BUCKETS - exact)).astype(np.int32)
    large = np.minimum(large, REL_BUCKETS - 1)
    return np.where(n < exact, n, large).astype(np.int32)


def _bias_kernel(tab_ref, bucket_ref, mask_ref, o_ref):
    h = pl.program_id(0)
    bucket = bucket_ref[...]
    acc = mask_ref[...]
    for b in range(REL_BUCKETS):
        acc = acc + jnp.where(bucket == b, tab_ref[b, h], 0.0)
    o_ref[0] = acc


def _rel_bias(rel_table, bucket, mask_add):
    rr, cc = bucket.shape
    return pl.pallas_call(
        _bias_kernel, grid=(SWA_HEADS,),
        in_specs=[pl.BlockSpec(memory_space=pltpu.SMEM), _full((rr, cc)), _full((rr, cc))],
        out_specs=pl.BlockSpec((1, rr, cc), lambda h: (h, 0, 0)),
        out_shape=jax.ShapeDtypeStruct((SWA_HEADS, rr, cc), F32),
        compiler_params=_params(("parallel",)),
        name="rel_bias",
    )(rel_table, bucket, mask_add)


def _sink_softmax(s, sink):
    m = jnp.maximum(jnp.max(s, axis=-1, keepdims=True), sink)
    e = jnp.exp(s - m)
    return e / (jnp.sum(e, axis=-1, keepdims=True) + jnp.exp(sink - m))


def _rms_matmul_bias_kernel(x_ref, g_ref, w_ref, b_ref, q_ref, k_ref, v_ref, *, nq, nk):
    h = _rms(x_ref[...], g_ref[...]).astype(BF16)
    z = _dot(h, w_ref[...]) + b_ref[...]
    q_ref[...] = z[:, :nq].astype(BF16)
    k_ref[...] = z[:, nq:nq + nk]
    v_ref[...] = z[:, nq + nk:]


def _swa_qkv(x, g, w, b, tm):
    m, d = x.shape
    nq = SWA_HEADS * SWA_HEAD_DIM
    nk = SWA_KV_HEADS * SWA_HEAD_DIM
    row = lambda i: (i, 0)
    return pl.pallas_call(
        functools.partial(_rms_matmul_bias_kernel, nq=nq, nk=nk),
        grid=(m // tm,),
        in_specs=[pl.BlockSpec((tm, d), row), _full(g.shape), _full(w.shape), _full(b.shape)],
        out_specs=[pl.BlockSpec((tm, nq), row), pl.BlockSpec((tm, nk), row), pl.BlockSpec((tm, nk), row)],
        out_shape=[jax.ShapeDtypeStruct((m, nq), BF16), jax.ShapeDtypeStruct((m, nk), F32),
                   jax.ShapeDtypeStruct((m, nk), F32)],
        compiler_params=_params(("parallel",)),
        name="swa_qkv",
    )(x, g, w, b)


def _swa_prompt_kernel(sink_ref, q_ref, kp_ref, kc_ref, vp_ref, vc_ref, bias_ref, o_ref, *, scale):
    q = q_ref[...]
    kk = jnp.concatenate([kp_ref[...], kc_ref[...]], axis=0).astype(BF16)
    vv = jnp.concatenate([vp_ref[...], vc_ref[...]], axis=0).astype(BF16)
    hd = SWA_HEAD_DIM
    w = q.shape[0]
    kvhs = range(SWA_KV_HEADS)
    qs = [jnp.concatenate([q[:, (kvh * SWA_GROUP + g) * hd:(kvh * SWA_GROUP + g + 1) * hd] for g in range(SWA_GROUP)],
                          axis=0) for kvh in kvhs]
    ss = [_dot_nt(qs[kvh], kk[:, kvh * hd:(kvh + 1) * hd]) for kvh in kvhs]
    ones = jnp.ones((2 * w, LANES), BF16)
    es, sink_terms = [], []
    for kvh in kvhs:
        sink = jnp.concatenate([jnp.full((w, LANES), sink_ref[kvh * SWA_GROUP + g], F32) for g in range(SWA_GROUP)],
                               axis=0)
        s = ss[kvh] * scale + bias_ref[0, kvh]
        m = jnp.maximum(jnp.broadcast_to(jnp.max(s, axis=-1, keepdims=True), sink.shape), sink)
        es.append(jnp.exp(s - jnp.concatenate([m] * (2 * w // LANES), axis=1)).astype(BF16))
        sink_terms.append(jnp.exp(sink - m))
    outs = []
    for kvh in kvhs:
        pair = kvh // 2 * 2 * hd
        oe = _dot(es[kvh], jnp.concatenate([vv[:, pair:pair + LANES], ones], axis=1))
        o = oe[:, :LANES] / (oe[:, LANES:] + sink_terms[kvh])
        half = kvh % 2 * hd
        outs += [o[g * w:(g + 1) * w, half:half + hd] for g in range(SWA_GROUP)]
    o_ref[...] = jnp.concatenate(outs, axis=1).astype(o_ref.dtype)


def _swa_prompt(q, k, v, bias2, sinks, batch, seq):
    bias2 = bias2.reshape(2, SWA_KV_HEADS, SWA_GROUP * WINDOW, 2 * WINDOW)
    m = q.shape[0]
    nb = seq // WINDOW
    nq = q.shape[1]
    nk = k.shape[1]
    scale = SWA_HEAD_DIM ** -0.5
    cur = lambda b, n: (b * nb + n, 0)
    prev = lambda b, n: (b * nb + jnp.maximum(n - 1, 0), 0)
    return pl.pallas_call(
        functools.partial(_swa_prompt_kernel, scale=scale),
        grid=(batch, nb),
        in_specs=[pl.BlockSpec(memory_space=pltpu.SMEM),
                  pl.BlockSpec((WINDOW, nq), cur),
                  pl.BlockSpec((WINDOW, nk), prev), pl.BlockSpec((WINDOW, nk), cur),
                  pl.BlockSpec((WINDOW, nk), prev), pl.BlockSpec((WINDOW, nk), cur),
                  pl.BlockSpec((1,) + bias2.shape[1:], lambda b, n: (jnp.minimum(n, 1), 0, 0, 0))],
        out_specs=pl.BlockSpec((WINDOW, nq), cur),
        out_shape=jax.ShapeDtypeStruct((m, nq), BF16),
        compiler_params=_params(("parallel", "arbitrary")),
        name="swa_prompt",
    )(sinks, q, k, k, v, v, bias2)


def _swa_sample_kernel(sink_ref, q_ref, kbt_ref, kn_ref, vbt_ref, vn_ref, bias_ref,
                       o_ref, kot_ref, vot_ref, *, bblk, t_new, nbuf, scale):
    hd = SWA_HEAD_DIM
    nk = kn_ref.shape[1]
    kvhs = range(SWA_KV_HEADS)
    sinks = [jnp.concatenate([jnp.full((t_new, 1), sink_ref[kvh * SWA_GROUP + g], F32) for g in range(SWA_GROUP)],
                             axis=0) for kvh in kvhs]
    lane = lax.broadcasted_iota(jnp.int32, (nk, nbuf), 1)

    def shift_in(buf_t, new):
        placed = jnp.concatenate([jnp.zeros((nbuf - t_new, nk), F32), new], axis=0)
        placed_t = jnp.concatenate([placed[:, c * nbuf:(c + 1) * nbuf].T for c in range(nk // nbuf)], axis=0)
        return jnp.where(lane >= nbuf - t_new, placed_t, pltpu.roll(buf_t, nbuf - t_new, 1))

    def one(bi, carry):
        row0 = pl.multiple_of(bi * t_new, t_new)
        q = q_ref[pl.ds(row0, t_new), :]
        kn = kn_ref[pl.ds(row0, t_new), :]
        vn = vn_ref[pl.ds(row0, t_new), :]
        kbt = kbt_ref[bi]
        vbt = vbt_ref[bi]
        kot_ref[bi] = shift_in(kbt, kn)
        vot_ref[bi] = shift_in(vbt, vn)
        kb16, vb16 = kbt.astype(BF16), vbt.astype(BF16)
        zpad = jnp.zeros((t_new, nk), F32)
        kn16 = jnp.concatenate([kn, zpad], axis=0).astype(BF16)
        vn16 = jnp.concatenate([vn, zpad], axis=0).astype(BF16)
        qs = [jnp.concatenate([q[:, (kvh * SWA_GROUP + g) * hd:(kvh * SWA_GROUP + g + 1) * hd]
                               for g in range(SWA_GROUP)], axis=0) for kvh in kvhs]
        sb = [_dot(qs[kvh], kb16[kvh * hd:(kvh + 1) * hd, :]) for kvh in kvhs]
        sn = [_dot_nt(qs[kvh], kn16[:, kvh * hd:(kvh + 1) * hd]) for kvh in kvhs]
        eb, en, rs = [], [], []
        for kvh in kvhs:
            s_b = sb[kvh] * scale + bias_ref[kvh, :, :nbuf]
            s_n = sn[kvh] * scale + bias_ref[kvh, :, nbuf:]
            m = jnp.maximum(jnp.maximum(jnp.max(s_b, axis=-1, keepdims=True), jnp.max(s_n, axis=-1, keepdims=True)),
                            sinks[kvh])
            e_b, e_n = jnp.exp(s_b - m), jnp.exp(s_n - m)
            rs.append(1.0 / (jnp.sum(e_b, axis=-1, keepdims=True) + jnp.sum(e_n, axis=-1, keepdims=True)
                             + jnp.exp(sinks[kvh] - m)))
            eb.append(e_b.astype(BF16))
            en.append(e_n.astype(BF16))
        outs = []
        for kvh in kvhs:
            o = (_dot_nt(eb[kvh], vb16[kvh * hd:(kvh + 1) * hd, :])
                 + _dot(en[kvh], vn16[:, kvh * hd:(kvh + 1) * hd])) * rs[kvh]
            outs += [o[g * t_new:(g + 1) * t_new] for g in range(SWA_GROUP)]
        o_ref[pl.ds(row0, t_new), :] = jnp.concatenate(outs, axis=1).astype(o_ref.dtype)
        return carry

    lax.fori_loop(0, bblk, one, 0)


def _swa_sample(q, k, v, buf_kt, buf_vt, bias, sinks, t_new, bblk):
    bd, nk, nbuf = buf_kt.shape
    assert bd % bblk == 0 and nbuf == LANES and nk % nbuf == 0 and bias.shape[2] == nbuf + 2 * t_new
    nq = q.shape[1]
    scale = SWA_HEAD_DIM ** -0.5
    rows = lambda i: (i, 0)
    bufs = lambda i: (i, 0, 0)
    return pl.pallas_call(
        functools.partial(_swa_sample_kernel, bblk=bblk, t_new=t_new, nbuf=nbuf, scale=scale),
        grid=(bd // bblk,),
        in_specs=[pl.BlockSpec(memory_space=pltpu.SMEM),
                  pl.BlockSpec((bblk * t_new, nq), rows),
                  pl.BlockSpec((bblk, nk, nbuf), bufs), pl.BlockSpec((bblk * t_new, nk), rows),
                  pl.BlockSpec((bblk, nk, nbuf), bufs), pl.BlockSpec((bblk * t_new, nk), rows),
                  _full(bias.shape)],
        out_specs=[pl.BlockSpec((bblk * t_new, nq), rows),
                   pl.BlockSpec((bblk, nk, nbuf), bufs), pl.BlockSpec((bblk, nk, nbuf), bufs)],
        out_shape=[jax.ShapeDtypeStruct((bd * t_new, nq), BF16),
                   jax.ShapeDtypeStruct((bd, nk, nbuf), F32), jax.ShapeDtypeStruct((bd, nk, nbuf), F32)],
        compiler_params=_params(("parallel",)),
        name="swa_sample",
    )(sinks, q, buf_kt, k, buf_vt, v, bias)


def _mlp_kernel(x_ref, g_ref, w1_ref, w2_ref, o_ref, h_ref):
    f = pl.program_id(1)

    @pl.when(f == 0)
    def _():
        x = x_ref[...]
        h_ref[...] = _rms(x, g_ref[...]).astype(BF16)
        o_ref[...] = x

    a = jnp.maximum(_dot(h_ref[...], w1_ref[...]), 0.0)
    o_ref[...] += _dot((a * a).astype(BF16), w2_ref[...])


def _mlp(x, g, w1, w2, layer, tm, tf):
    m, d = x.shape
    ff = w1.shape[2]
    return pl.pallas_call(
        _mlp_kernel, grid=(m // tm, ff // tf),
        in_specs=[pl.BlockSpec((tm, d), lambda i, f: (i, 0)), pl.BlockSpec((1, d), lambda i, f: (0, 0)),
                  pl.BlockSpec((None, d, tf), lambda i, f: (layer, 0, f)),
                  pl.BlockSpec((None, tf, d), lambda i, f: (layer, f, 0))],
        out_specs=pl.BlockSpec((tm, d), lambda i, f: (i, 0)),
        out_shape=jax.ShapeDtypeStruct((m, d), F32),
        scratch_shapes=[pltpu.VMEM((tm, d), BF16)],
        compiler_params=_params(("parallel", "arbitrary")),
        name="mlp",
    )(x, g, w1, w2)


def _ple_kernel(x_ref, g_ref, wg_ref, p_ref, wp_ref, *rest, final):
    if final:
        gf_ref, o_ref = rest
    else:
        (o_ref,) = rest
    x = x_ref[...]
    gate = jax.nn.sigmoid(_dot(_rms(x, g_ref[...]).astype(BF16), wg_ref[...]))
    y = x + gate * _dot(p_ref[...].astype(BF16), wp_ref[...])
    if final:
        y = _rms(y, gf_ref[...])
    o_ref[...] = y


def _ple(x, g, w_gate, p, layer, w_proj, tm, g_final=None):
    m, d = x.shape
    pd = p.shape[2]
    row = lambda i: (i, 0)
    final = g_final is not None
    in_specs = [pl.BlockSpec((tm, d), row), _full(g.shape), _full(w_gate.shape),
                pl.BlockSpec((None, tm, pd), lambda i: (layer, i, 0)), _full(w_proj.shape)]
    args = [x, g, w_gate, p, w_proj]
    if final:
        in_specs.append(_full(g_final.shape))
        args.append(g_final)
    return pl.pallas_call(
        functools.partial(_ple_kernel, final=final), grid=(m // tm,),
        in_specs=in_specs, out_specs=pl.BlockSpec((tm, d), row),
        out_shape=jax.ShapeDtypeStruct((m, d), F32),
        compiler_params=_params(("parallel",)),
        name="ple_final" if final else "ple",
    )(*args)


def _rope_tables(pos):
    half = ROPE_DIM // 2
    inv = ROPE_BASE ** (-jnp.arange(half, dtype=F32) / half)
    ang = pos.astype(F32)[:, None] * inv[None, :]
    reps = LANES // half
    return jnp.tile(jnp.cos(ang), (1, reps)), jnp.tile(jnp.sin(ang), (1, reps))


def _rotate_half_cols(w):
    half = ROPE_DIM // 2
    return jnp.concatenate([-w[..., half:], w[..., :half]], axis=-1)


def _prep_mla_weights(w_in, w_qb, w_kvb, q_lora, kv_lora):
    d = w_in.shape[0]
    o = q_lora + kv_lora
    w_r = w_in[:, o:o + ROPE_DIM]
    zpad = jnp.zeros((d, LANES - ROPE_DIM), w_in.dtype)
    w_ext = jnp.concatenate([w_in[:, :o], w_r, zpad, _rotate_half_cols(w_r), zpad], axis=1).astype(BF16)
    wq = w_qb.reshape(q_lora, MLA_HEADS, NOPE_DIM + ROPE_DIM)
    zq = jnp.zeros((q_lora, MLA_HEADS, MLA_QK_PAD - NOPE_DIM - ROPE_DIM), w_qb.dtype)
    wq_pad = jnp.concatenate([wq, zq], axis=2).reshape(q_lora, MLA_HEADS * MLA_QK_PAD).astype(BF16)
    wrot = jnp.concatenate([_rotate_half_cols(wq[..., NOPE_DIM:]), zq], axis=2)
    wrot = wrot.reshape(q_lora, MLA_HEADS * LANES).astype(BF16)
    wkv = w_kvb.reshape(kv_lora, MLA_HEADS, NOPE_DIM + V_DIM)
    wuk_t = jnp.transpose(wkv[..., :NOPE_DIM], (1, 2, 0)).astype(BF16)
    wuv = jnp.transpose(wkv[..., NOPE_DIM:], (1, 0, 2)).astype(BF16)
    return w_ext, wq_pad, wrot, wuk_t, wuv


def _swa_bias_tables(rel_bias, t_new, nbuf, kpad):
    w = WINDOW
    qi = np.arange(w)
    kj = np.arange(2 * w)
    dist = (w + qi[:, None]) - kj[None, :]
    ok = (dist >= 0) & (dist < w)
    mask1 = np.where(ok, 0.0, NEG_INF).astype(np.float32)
    mask0 = np.where(ok & (kj[None, :] >= w), 0.0, NEG_INF).astype(np.float32)
    bucket = _t5_bucket_np(dist)
    bias2 = jnp.stack([_rel_bias(rel_bias, jnp.asarray(bucket), jnp.asarray(mask0)),
                       _rel_bias(rel_bias, jnp.asarray(bucket), jnp.asarray(mask1))])
    kj = np.arange(kpad)
    dist_s = (nbuf + np.arange(t_new))[:, None] - kj[None, :]
    ok_s = (dist_s >= 0) & (dist_s < w) & (kj[None, :] < nbuf + t_new)
    bias_s = _rel_bias(rel_bias, jnp.asarray(_t5_bucket_np(dist_s)),
                       jnp.asarray(np.where(ok_s, 0.0, NEG_INF).astype(np.float32)))
    bias_s = bias_s.reshape(SWA_KV_HEADS, SWA_GROUP * t_new, kpad)
    return bias2, bias_s


def kernel(x_prompt, x_sample, cache_mla_latent, cache_mla_rope, cache_swa_k, cache_swa_v, page_table, p_prompt, p_sample, norm_mix, norm_mlp, norm_ple, norm_final, mla_w_in, mla_g_q, mla_g_kv, mla_w_qb, mla_w_kvb, mla_w_o, swa_w_qkv, swa_b_qkv, swa_sinks, swa_w_o, rel_bias, mlp_w1, mlp_w2, ple_w_gate, ple_w_proj):
    batch, seq, d = x_prompt.shape
    bd, t_new, _ = x_sample.shape
    depth = norm_mix.shape[0]
    q_lora = mla_g_q.shape[1]
    kv_lora = mla_g_kv.shape[1]
    past = page_table.shape[1] * PAGE_SIZE
    nbuf = cache_swa_k.shape[2]
    mp, ms = batch * seq, bd * t_new
    mla_scale = (NOPE_DIM + ROPE_DIM) ** -0.5
    tmp = _row_tile(mp, 512)
    tms = _row_tile(ms, 512)

    xp = x_prompt.reshape(mp, d)
    xs = x_sample.reshape(ms, d)
    row2 = lambda a: a.reshape(1, -1)

    cos_p, sin_p = _rope_tables(jnp.arange(seq))
    cos_s, sin_s = _rope_tables(past + jnp.arange(t_new))
    cos_s, sin_s = jnp.tile(cos_s, (tms // t_new, 1)), jnp.tile(sin_s, (tms // t_new, 1))

    bias2, bias_s = _swa_bias_tables(rel_bias, t_new, nbuf, nbuf + 2 * t_new)

    w1_all, w2_all = mlp_w1.astype(BF16), mlp_w2.astype(BF16)
    outs = {k: [] for k in ("lat_p", "rope_p", "lat_s", "rope_s", "kp", "vp", "ks", "vs")}
    for i in range(depth):
        j = i // 2
        g_mix = row2(norm_mix[i])
        if i % 2 == 0:
            w_ext, wq_pad, wrot, wuk_t, wuv = _prep_mla_weights(mla_w_in[j], mla_w_qb[j], mla_w_kvb[j],
                                                                q_lora, kv_lora)
            gq, gkv = row2(mla_g_q[j]), row2(mla_g_kv[j])
            w_o = mla_w_o[j].astype(BF16)
            cq, ckv, ckvb, kr, krp = _mla_in(xp, g_mix, w_ext, gq, gkv, cos_p, sin_p, tmp)
            q = _mla_q(cq, wq_pad, wrot, cos_p, sin_p, tmp, mla_scale * math.log2(math.e))[0]
            kv = _matmul(ckvb, mla_w_kvb[j].astype(BF16), tmp, BF16)
            o = _mla_flash(q, kv, krp, batch, seq, _row_tile(seq, 4096), _row_tile(seq, 512))
            xp = _matmul_res(xp, o, w_o, tmp)
            outs["lat_p"].append(ckv.reshape(batch, seq, kv_lora))
            outs["rope_p"].append(kr.reshape(batch, seq, ROPE_DIM))
            cq, ckv, ckvb, kr, krp = _mla_in(xs, g_mix, w_ext, gq, gkv, cos_s, sin_s, tms)
            q, qlat = _mla_q(cq, wq_pad, wrot, cos_s, sin_s, tms, 1.0, wuk_t)
            q4 = q.reshape(bd, t_new, MLA_HEADS, MLA_QK_PAD)[..., NOPE_DIM:NOPE_DIM + ROPE_DIM]
            qt = jnp.concatenate([jnp.transpose(qlat.reshape(bd, t_new, MLA_HEADS, kv_lora), (0, 3, 2, 1)),
                                  jnp.transpose(q4, (0, 3, 2, 1))], axis=1)
            qt = qt.reshape(bd, kv_lora + ROPE_DIM, MLA_HEADS * t_new)
            knew = jnp.concatenate([ckvb.reshape(bd, t_new, kv_lora),
                                    krp[:, :ROPE_DIM].reshape(bd, t_new, ROPE_DIM)], axis=2)
            knew = jnp.pad(knew, ((0, 0), (0, MLA_HEADS * t_new - t_new), (0, 0)))
            o_lat = _mla_decode(page_table, qt, knew, cache_mla_latent, jnp.swapaxes(cache_mla_rope, 2, 3),
                                j, _row_tile(page_table.shape[1], 32), 8)
            o_lat = jnp.transpose(o_lat.reshape(bd, MLA_HEADS, t_new, kv_lora), (1, 0, 2, 3))
            o = _head_matmul(o_lat.reshape(MLA_HEADS, ms, kv_lora), wuv)
            xs = _matmul_res(xs, o, w_o, tms)
            outs["lat_s"].append(ckv.reshape(bd, t_new, kv_lora))
            outs["rope_s"].append(kr.reshape(bd, t_new, ROPE_DIM))
        else:
            w_qkv = swa_w_qkv[j].astype(BF16)
            b_qkv = row2(swa_b_qkv[j])
            w_o = swa_w_o[j].astype(BF16)
            sinks = swa_sinks[j]
            nk = SWA_KV_HEADS * SWA_HEAD_DIM
            q, k, v = _swa_qkv(xp, g_mix, w_qkv, b_qkv, tmp)
            o = _swa_prompt(q, k, v, bias2, sinks, batch, seq)
            xp = _matmul_res(xp, o, w_o, tmp)
            tail = lambda a: a.reshape(batch, seq, nk)[:, -WINDOW:].reshape(batch, WINDOW, SWA_KV_HEADS, SWA_HEAD_DIM)
            outs["kp"].append(tail(k))
            outs["vp"].append(tail(v))
            q, k, v = _swa_qkv(xs, g_mix, w_qkv, b_qkv, tms)
            feat_major = lambda c: jnp.swapaxes(c.reshape(bd, nbuf, nk), 1, 2)
            o, k_new, v_new = _swa_sample(q, k, v, feat_major(cache_swa_k[j]), feat_major(cache_swa_v[j]),
                                          bias_s, sinks, t_new, _row_tile(bd, 8))
            xs = _matmul_res(xs, o, w_o, tms)
            pos_major = lambda c: jnp.swapaxes(c, 1, 2).reshape(bd, nbuf, SWA_KV_HEADS, SWA_HEAD_DIM)
            outs["ks"].append(pos_major(k_new))
            outs["vs"].append(pos_major(v_new))
        wg, wp = ple_w_gate[i].astype(BF16), ple_w_proj[i].astype(BF16)
        g_mlp, g_ple = row2(norm_mlp[i]), row2(norm_ple[i])
        g_fin = row2(norm_final) if i == depth - 1 else None
        tf = _row_tile(w1_all.shape[2], 1024)
        xp = _mlp(xp, g_mlp, w1_all, w2_all, i, tmp, tf)
        xs = _mlp(xs, g_mlp, w1_all, w2_all, i, tms, tf)
        xp = _ple(xp, g_ple, wg, p_prompt.reshape(depth, mp, -1), i, wp, tmp, g_fin)
        xs = _ple(xs, g_ple, wg, p_sample.reshape(depth, ms, -1), i, wp, tms, g_fin)

    st = lambda k: jnp.stack(outs[k])
    return (xp.reshape(batch, seq, d), xs.reshape(bd, t_new, d),
            st("lat_p"), st("rope_p"), st("lat_s"), st("rope_s"),
            st("kp"), st("vp"), st("ks"), st("vs"))
```

```python
import functools
import math

import numpy as np
import jax
import jax.numpy as jnp
from jax import lax
from jax.experimental import pallas as pl
from jax.experimental.pallas import tpu as pltpu

F32 = jnp.float32
BF16 = jnp.bfloat16

MLA_HEADS = 16
NOPE_DIM = 128
ROPE_DIM = 64
V_DIM = 128
ROPE_BASE = 10000.0
SWA_HEADS = 32
SWA_KV_HEADS = 8
SWA_GROUP = SWA_HEADS // SWA_KV_HEADS
SWA_HEAD_DIM = 64
WINDOW = 128
REL_BUCKETS = 32
REL_MAX_DIST = 128
PAGE_SIZE = 128
EPS = 1e-6

LANES = 128
MLA_QK_PAD = 2 * LANES
VMEM_LIMIT = 56 * 1024 * 1024
NEG_INF = float("-inf")
LOG2E = math.log2(math.e)


def _params(sem):
    return pltpu.CompilerParams(dimension_semantics=sem, vmem_limit_bytes=VMEM_LIMIT)


def _rms(xf, g):
    return xf * lax.rsqrt(jnp.mean(xf * xf, axis=-1, keepdims=True) + EPS) * g


def _dot(a, b):
    return jnp.dot(a, b, preferred_element_type=F32)


def _dot_nt(a, b):
    return lax.dot_general(a, b, (((1,), (1,)), ((), ())), preferred_element_type=F32)


def _row_tile(m, want):
    t = min(m, want)
    assert m % t == 0
    return t


def _full(shape):
    return pl.BlockSpec(shape, lambda *_: (0,) * len(shape))


def _mla_in_kernel(x_ref, g_ref, w_ref, gq_ref, gkv_ref, cos_ref, sin_ref,
                   cq_ref, ckv_ref, ckvb_ref, kr_ref, krp_ref, *, q_lora, kv_lora):
    h = _rms(x_ref[...], g_ref[...]).astype(BF16)
    z = _dot(h, w_ref[...])
    cq_ref[...] = _rms(z[:, :q_lora], gq_ref[...]).astype(BF16)
    ckv = _rms(z[:, q_lora:q_lora + kv_lora], gkv_ref[...])
    ckv_ref[...] = ckv
    ckvb_ref[...] = ckv.astype(BF16)
    o = q_lora + kv_lora
    kr = z[:, o:o + LANES] * cos_ref[...] + z[:, o + LANES:o + 2 * LANES] * sin_ref[...]
    kr_ref[...] = kr[:, :ROPE_DIM]
    krp_ref[...] = kr.astype(BF16)


def _mla_in(x, g, w_ext, gq, gkv, cos, sin, tm):
    m, d = x.shape
    q_lora, kv_lora = gq.shape[1], gkv.shape[1]
    nrep = cos.shape[0] // tm
    row = lambda i: (i, 0)
    tab = lambda i: (i % nrep, 0)
    return pl.pallas_call(
        functools.partial(_mla_in_kernel, q_lora=q_lora, kv_lora=kv_lora),
        grid=(m // tm,),
        in_specs=[pl.BlockSpec((tm, d), row), _full(g.shape), _full(w_ext.shape),
                  _full(gq.shape), _full(gkv.shape),
                  pl.BlockSpec((tm, LANES), tab), pl.BlockSpec((tm, LANES), tab)],
        out_specs=[pl.BlockSpec((tm, q_lora), row), pl.BlockSpec((tm, kv_lora), row),
                   pl.BlockSpec((tm, kv_lora), row), pl.BlockSpec((tm, ROPE_DIM), row),
                   pl.BlockSpec((tm, LANES), row)],
        out_shape=[jax.ShapeDtypeStruct((m, q_lora), BF16), jax.ShapeDtypeStruct((m, kv_lora), F32),
                   jax.ShapeDtypeStruct((m, kv_lora), BF16), jax.ShapeDtypeStruct((m, ROPE_DIM), F32),
                   jax.ShapeDtypeStruct((m, LANES), BF16)],
        compiler_params=_params(("parallel",)),
        name="mla_in",
    )(x, g, w_ext, gq, gkv, cos, sin)


def _mla_q_kernel(cq_ref, wq_ref, wrot_ref, cos_ref, sin_ref, *rest, absorb, q_scale):
    if absorb:
        wuk_ref, q_ref, qlat_ref = rest
    else:
        (q_ref,) = rest
    cq = cq_ref[...]
    cos = cos_ref[...] * q_scale
    sin = sin_ref[...] * q_scale
    kv_lora = wuk_ref.shape[2] if absorb else 0
    for h in range(MLA_HEADS):
        a = _dot(cq, wq_ref[:, h * MLA_QK_PAD:(h + 1) * MLA_QK_PAD])
        r = _dot(cq, wrot_ref[:, h * LANES:(h + 1) * LANES])
        qn = (a[:, :NOPE_DIM] * q_scale).astype(BF16)
        qr = (a[:, NOPE_DIM:] * cos + r * sin).astype(BF16)
        q_ref[:, h * MLA_QK_PAD:h * MLA_QK_PAD + NOPE_DIM] = qn
        q_ref[:, h * MLA_QK_PAD + NOPE_DIM:(h + 1) * MLA_QK_PAD] = qr
        if absorb:
            qlat_ref[:, h * kv_lora:(h + 1) * kv_lora] = _dot(qn, wuk_ref[h]).astype(BF16)


def _mla_q(cq, wq_pad, wrot, cos, sin, tm, q_scale, wuk_t=None):
    m, q_lora = cq.shape
    nrep = cos.shape[0] // tm
    row = lambda i: (i, 0)
    tab = lambda i: (i % nrep, 0)
    absorb = wuk_t is not None
    in_specs = [pl.BlockSpec((tm, q_lora), row), _full(wq_pad.shape), _full(wrot.shape),
                pl.BlockSpec((tm, LANES), tab), pl.BlockSpec((tm, LANES), tab)]
    out_specs = [pl.BlockSpec((tm, MLA_HEADS * MLA_QK_PAD), row)]
    out_shape = [jax.ShapeDtypeStruct((m, MLA_HEADS * MLA_QK_PAD), BF16)]
    args = [cq, wq_pad, wrot, cos, sin]
    if absorb:
        kv_lora = wuk_t.shape[2]
        in_specs.append(_full(wuk_t.shape))
        out_specs.append(pl.BlockSpec((tm, MLA_HEADS * kv_lora), row))
        out_shape.append(jax.ShapeDtypeStruct((m, MLA_HEADS * kv_lora), BF16))
        args.append(wuk_t)
    return pl.pallas_call(
        functools.partial(_mla_q_kernel, absorb=absorb, q_scale=q_scale),
        grid=(m // tm,), in_specs=in_specs, out_specs=out_specs, out_shape=out_shape,
        compiler_params=_params(("parallel",)),
        name="mla_q_absorb" if absorb else "mla_q",
    )(*args)


def _matmul_kernel(a_ref, w_ref, o_ref):
    o_ref[...] = _dot(a_ref[...], w_ref[...]).astype(o_ref.dtype)


def _matmul(a, w, tm, out_dtype):
    m, k = a.shape
    n = w.shape[1]
    return pl.pallas_call(
        _matmul_kernel, grid=(m // tm,),
        in_specs=[pl.BlockSpec((tm, k), lambda i: (i, 0)), _full(w.shape)],
        out_specs=pl.BlockSpec((tm, n), lambda i: (i, 0)),
        out_shape=jax.ShapeDtypeStruct((m, n), out_dtype),
        compiler_params=_params(("parallel",)),
        name="matmul",
    )(a, w)


def _matmul_res_kernel(x_ref, a_ref, w_ref, o_ref):
    o_ref[...] = x_ref[...] + _dot(a_ref[...], w_ref[...])


def _matmul_res(x, a, w, tm):
    m, k = a.shape
    n = w.shape[1]
    return pl.pallas_call(
        _matmul_res_kernel, grid=(m // tm,),
        in_specs=[pl.BlockSpec((tm, n), lambda i: (i, 0)), pl.BlockSpec((tm, k), lambda i: (i, 0)),
                  _full(w.shape)],
        out_specs=pl.BlockSpec((tm, n), lambda i: (i, 0)),
        out_shape=jax.ShapeDtypeStruct((m, n), F32),
        compiler_params=_params(("parallel",)),
        name="matmul_res",
    )(x, a, w)


def _head_matmul_kernel(a_ref, w_ref, o_ref):
    o_ref[...] = _dot(a_ref[...], w_ref[0]).astype(o_ref.dtype)


def _head_matmul(a, w):
    hh, k, n = w.shape
    m = a.shape[0]
    assert a.shape[1] == hh * k
    return pl.pallas_call(
        _head_matmul_kernel, grid=(hh,),
        in_specs=[pl.BlockSpec((m, k), lambda h: (0, h)), pl.BlockSpec((1, k, n), lambda h: (h, 0, 0))],
        out_specs=pl.BlockSpec((m, n), lambda h: (0, h)),
        out_shape=jax.ShapeDtypeStruct((m, hh * n), BF16),
        compiler_params=_params(("parallel",)),
        name="head_matmul",
    )(a, w)


def _mla_flash_kernel(q_ref, kn_ref, kr_ref, v_ref, o_ref, vt_ref, *, tq, tk):
    i = pl.program_id(2)
    per_q = tq // tk

    @pl.when(i == 0)
    def _():
        for jj in range(vt_ref.shape[0]):
            vt_ref[jj] = v_ref[jj * tk:(jj + 1) * tk, :].astype(F32).T.astype(BF16)

    qt = q_ref[...].astype(F32).T.astype(BF16)

    def keys(j):
        off = pl.multiple_of(j * tk, tk)
        return jnp.concatenate([kn_ref[pl.ds(off, tk), :], kr_ref[pl.ds(off, tk), :]], axis=1)

    def update(j, st, carry):
        m, l, acc = carry
        m_new = jnp.maximum(m, jnp.max(st, axis=0, keepdims=True))
        alpha = jnp.exp2(m - m_new)
        pt = jnp.exp2(st - m_new)
        l = alpha * l + jnp.sum(pt, axis=0, keepdims=True)
        acc = alpha * acc + _dot(vt_ref[j], pt.astype(BF16))
        return m_new, l, acc

    def body(jj, carry):
        sts = [_dot(keys(jj * per_q + u), qt) for u in range(per_q)]
        for u in range(per_q):
            carry = update(jj * per_q + u, sts[u], carry)
        return carry

    init = (jnp.full((1, tq), NEG_INF, F32), jnp.zeros((1, tq), F32), jnp.zeros((V_DIM, tq), F32))
    carry = lax.fori_loop(0, i, body, init)
    for d in range(per_q):
        c0 = d * tk
        key = lax.broadcasted_iota(jnp.int32, (tk, tq - c0), 0)
        qry = lax.broadcasted_iota(jnp.int32, (tk, tq - c0), 1)
        st = jnp.where(key <= qry, _dot(keys(i * per_q + d), qt[:, c0:]), NEG_INF)
        sub = update(i * per_q + d, st, tuple(x[:, c0:] for x in carry))
        carry = tuple(jnp.concatenate([x[:, :c0], y], axis=1) if c0 else y for x, y in zip(carry, sub))
    m, l, acc = carry
    o_ref[...] = (acc / l).T.astype(o_ref.dtype)


def _mla_flash(q, kv, krp, batch, seq, tq, tk):
    m = q.shape[0]
    nq = seq // tq
    assert tq % tk == 0 and tk % LANES == 0
    return pl.pallas_call(
        functools.partial(_mla_flash_kernel, tq=tq, tk=tk),
        grid=(batch, MLA_HEADS, nq),
        in_specs=[pl.BlockSpec((tq, MLA_QK_PAD), lambda b, h, i: (b * nq + i, h)),
                  pl.BlockSpec((seq, NOPE_DIM), lambda b, h, i: (b, 2 * h)),
                  pl.BlockSpec((seq, LANES), lambda b, h, i: (b, 0)),
                  pl.BlockSpec((seq, V_DIM), lambda b, h, i: (b, 2 * h + 1))],
        out_specs=pl.BlockSpec((tq, V_DIM), lambda b, h, i: (b * nq + i, h)),
        out_shape=jax.ShapeDtypeStruct((m, MLA_HEADS * V_DIM), BF16),
        scratch_shapes=[pltpu.VMEM((seq // tk, V_DIM, tk), BF16)],
        compiler_params=_params(("parallel", "parallel", "arbitrary")),
        name="mla_flash",
    )(q, kv, krp, kv)


def _col(row):
    n = row.shape[1]
    return jnp.broadcast_to(row, (n, n)).T


def _scale_rows(x, col):
    n = col.shape[0]
    return jnp.concatenate([x[:, c * n:(c + 1) * n] * col for c in range(x.shape[1] // n)], axis=1)


def _mla_decode_kernel(pt_ref, ql_ref, qr_ref, kn_ref, lat_hbm, ropet_hbm, o_ref,
                       lat_buf, rope_buf, sem, m_ref, l_ref, acc_ref,
                       *, layer, n_pg, n_sub, n_chunk, t_new, scale):
    b = pl.program_id(0)
    nb = pl.num_programs(0)
    c_dim = lat_buf.shape[2]

    def page_copies(bb, ch, slot):
        cps = []
        for g in range(n_pg):
            page = pt_ref[bb, ch * n_pg + g]
            cps.append(pltpu.make_async_copy(lat_hbm.at[layer, page],
                                             lat_buf.at[slot, pl.ds(g * PAGE_SIZE, PAGE_SIZE)], sem.at[0, slot]))
            cps.append(pltpu.make_async_copy(ropet_hbm.at[layer, page], rope_buf.at[slot, g], sem.at[1, slot]))
        return cps

    @pl.when(b == 0)
    def _():
        for cp in page_copies(0, 0, 0):
            cp.start()

    m_ref[...] = jnp.full(m_ref.shape, NEG_INF, F32)
    l_ref[...] = jnp.zeros(l_ref.shape, F32)
    acc_ref[...] = jnp.zeros(acc_ref.shape, F32)
    qt_lat = ql_ref[0].astype(F32).T.astype(BF16)
    qt_rope = qr_ref[0].astype(F32).T.astype(BF16)

    def attend(kc, st):
        m_old = m_ref[...]
        m_new = jnp.maximum(m_old, jnp.max(st, axis=0, keepdims=True))
        alpha = jnp.exp(m_old - m_new)
        pt = jnp.exp(st - m_new)
        l_ref[...] = alpha * l_ref[...] + jnp.sum(pt, axis=0, keepdims=True)
        m_ref[...] = m_new
        pv = _dot(pt.T.astype(BF16), kc)
        acc_ref[...] = _scale_rows(acc_ref[...], _col(alpha)) + pv

    pg_sub = n_pg // n_sub

    def chunk(ch, carry):
        slot = ch % 2
        last = ch + 1 == n_chunk
        nxt_b = jnp.where(last, b + 1, b)
        nxt_ch = jnp.where(last, 0, ch + 1)

        @pl.when(nxt_b < nb)
        def _():
            for cp in page_copies(nxt_b, nxt_ch, 1 - slot):
                cp.start()

        for cp in page_copies(b, ch, slot):
            cp.wait()
        kcs, sts = [], []
        for u in range(n_sub):
            kc = lat_buf[slot, u * pg_sub * PAGE_SIZE:(u + 1) * pg_sub * PAGE_SIZE, :].astype(BF16)
            kr = jnp.concatenate([rope_buf[slot, u * pg_sub + g].T for g in range(pg_sub)], axis=0).astype(BF16)
            kcs.append(kc)
            sts.append((_dot(kc, qt_lat) + _dot(kr, qt_rope)) * scale)
        for u in range(n_sub):
            attend(kcs[u], sts[u])
        return carry

    lax.fori_loop(0, n_chunk, chunk, 0)

    kn = kn_ref[0]
    sn = (_dot(kn[:, :c_dim], qt_lat) + _dot(kn[:, c_dim:], qt_rope)) * scale
    u = lax.broadcasted_iota(jnp.int32, sn.shape, 0)
    t = lax.broadcasted_iota(jnp.int32, sn.shape, 1) // (sn.shape[1] // t_new)
    attend(kn[:, :c_dim], jnp.where(u <= t, sn, NEG_INF))
    o_ref[0] = _scale_rows(acc_ref[...], _col(1.0 / l_ref[...])).astype(o_ref.dtype)


def _mla_decode(page_table, qlat, qrope, knew, lat_pool, ropet_pool, layer, n_pg, n_sub):
    bd, rows, c_dim = qlat.shape
    r_dim = ropet_pool.shape[2]
    cr = c_dim + r_dim
    assert lat_pool.shape[3] == c_dim and qrope.shape == (bd, rows, r_dim)
    assert knew.shape == (bd, rows, cr) and rows == LANES
    t_new = rows // MLA_HEADS
    n_pages = page_table.shape[1]
    n_chunk = n_pages // n_pg
    assert n_pages == n_chunk * n_pg and n_chunk % 2 == 0 and n_pg % n_sub == 0
    scale = (NOPE_DIM + ROPE_DIM) ** -0.5
    per_b = lambda b, pt: (b, 0, 0)
    grid_spec = pltpu.PrefetchScalarGridSpec(
        num_scalar_prefetch=1, grid=(bd,),
        in_specs=[pl.BlockSpec((1, rows, c_dim), per_b), pl.BlockSpec((1, rows, r_dim), per_b),
                  pl.BlockSpec((1, rows, cr), per_b),
                  pl.BlockSpec(memory_space=pl.ANY), pl.BlockSpec(memory_space=pl.ANY)],
        out_specs=pl.BlockSpec((1, rows, c_dim), per_b),
        scratch_shapes=[pltpu.VMEM((2, n_pg * PAGE_SIZE, c_dim), F32),
                        pltpu.VMEM((2, n_pg, r_dim, PAGE_SIZE), F32),
                        pltpu.SemaphoreType.DMA((2, 2)),
                        pltpu.VMEM((1, rows), F32), pltpu.VMEM((1, rows), F32), pltpu.VMEM((rows, c_dim), F32)])
    return pl.pallas_call(
        functools.partial(_mla_decode_kernel, layer=layer, n_pg=n_pg, n_sub=n_sub, n_chunk=n_chunk, t_new=t_new,
                          scale=scale),
        grid_spec=grid_spec,
        out_shape=jax.ShapeDtypeStruct((bd, rows, c_dim), BF16),
        compiler_params=_params(("arbitrary",)),
        name="mla_decode",
    )(page_table, qlat, qrope, knew, lat_pool, ropet_pool)


def _t5_bucket_np(dist):
    n = np.maximum(dist, 0)
    exact = REL_BUCKETS // 2
    nf = np.maximum(n, 1).astype(np.float32)
    large = exact + (np.log(nf / np.float32(exact)) / np.float32(math.log(REL_MAX_DIST / exact))
                     * np.float32(REL_BUCKETS - exact)).astype(np.int32)
    large = np.minimum(large, REL_BUCKETS - 1)
    return np.where(n < exact, n, large).astype(np.int32)


def _bias_kernel(tab_ref, bucket_ref, mask_ref, o_ref):
    h = pl.program_id(0)
    bucket = bucket_ref[...]
    acc = mask_ref[...]
    for b in range(REL_BUCKETS):
        acc = acc + jnp.where(bucket == b, tab_ref[b, h], 0.0)
    o_ref[0] = acc * LOG2E


def _rel_bias(rel_table, bucket, mask_add):
    rr, cc = bucket.shape
    return pl.pallas_call(
        _bias_kernel, grid=(SWA_HEADS,),
        in_specs=[pl.BlockSpec(memory_space=pltpu.SMEM), _full((rr, cc)), _full((rr, cc))],
        out_specs=pl.BlockSpec((1, rr, cc), lambda h: (h, 0, 0)),
        out_shape=jax.ShapeDtypeStruct((SWA_HEADS, rr, cc), F32),
        compiler_params=_params(("parallel",)),
        name="rel_bias",
    )(rel_table, bucket, mask_add)


def _rms_matmul_bias_kernel(x_ref, g_ref, w_ref, b_ref, q_ref, k_ref, v_ref, *, nq, nk):
    h = _rms(x_ref[...], g_ref[...]).astype(BF16)
    z = _dot(h, w_ref[...]) + b_ref[...]
    q_ref[...] = z[:, :nq].astype(BF16)
    k_ref[...] = z[:, nq:nq + nk]
    v_ref[...] = z[:, nq + nk:]


def _swa_qkv(x, g, w, b, tm):
    m, d = x.shape
    nq = SWA_HEADS * SWA_HEAD_DIM
    nk = SWA_KV_HEADS * SWA_HEAD_DIM
    row = lambda i: (i, 0)
    return pl.pallas_call(
        functools.partial(_rms_matmul_bias_kernel, nq=nq, nk=nk),
        grid=(m // tm,),
        in_specs=[pl.BlockSpec((tm, d), row), _full(g.shape), _full(w.shape), _full(b.shape)],
        out_specs=[pl.BlockSpec((tm, nq), row), pl.BlockSpec((tm, nk), row), pl.BlockSpec((tm, nk), row)],
        out_shape=[jax.ShapeDtypeStruct((m, nq), BF16), jax.ShapeDtypeStruct((m, nk), F32),
                   jax.ShapeDtypeStruct((m, nk), F32)],
        compiler_params=_params(("parallel",)),
        name="swa_qkv",
    )(x, g, w, b)


def _swa_prompt_kernel(sink_ref, q_ref, kp_ref, kc_ref, vp_ref, vc_ref, bias_ref, o_ref, *, scale):
    q = q_ref[...]
    kk = jnp.concatenate([kp_ref[...], kc_ref[...]], axis=0).astype(BF16)
    vv = jnp.concatenate([vp_ref[...], vc_ref[...]], axis=0).astype(BF16)
    hd = SWA_HEAD_DIM
    w = q.shape[0]
    kvhs = range(SWA_KV_HEADS)
    qs = [jnp.concatenate([q[:, (kvh * SWA_GROUP + g) * hd:(kvh * SWA_GROUP + g + 1) * hd] for g in range(SWA_GROUP)],
                          axis=0) for kvh in kvhs]
    ss = [_dot_nt(qs[kvh], kk[:, kvh * hd:(kvh + 1) * hd]) for kvh in kvhs]
    ones = jnp.ones((2 * w, LANES), BF16)
    es, sink_terms = [], []
    for kvh in kvhs:
        sink = jnp.concatenate([jnp.full((w, LANES), sink_ref[kvh * SWA_GROUP + g] * LOG2E, F32)
                                for g in range(SWA_GROUP)], axis=0)
        s = ss[kvh] * (scale * LOG2E) + bias_ref[0, kvh]
        m = jnp.maximum(jnp.broadcast_to(jnp.max(s, axis=-1, keepdims=True), sink.shape), sink)
        es.append(jnp.exp2(s - jnp.concatenate([m] * (2 * w // LANES), axis=1)).astype(BF16))
        sink_terms.append(jnp.exp2(sink - m))
    outs = []
    for kvh in kvhs:
        pair = kvh // 2 * 2 * hd
        oe = _dot(es[kvh], jnp.concatenate([vv[:, pair:pair + LANES], ones], axis=1))
        o = oe[:, :LANES] / (oe[:, LANES:] + sink_terms[kvh])
        half = kvh % 2 * hd
        outs += [o[g * w:(g + 1) * w, half:half + hd] for g in range(SWA_GROUP)]
    o_ref[...] = jnp.concatenate(outs, axis=1).astype(o_ref.dtype)


def _swa_prompt(q, k, v, bias2, sinks, batch, seq):
    bias2 = bias2.reshape(2, SWA_KV_HEADS, SWA_GROUP * WINDOW, 2 * WINDOW)
    m = q.shape[0]
    nb = seq // WINDOW
    nq = q.shape[1]
    nk = k.shape[1]
    scale = SWA_HEAD_DIM ** -0.5
    cur = lambda b, n: (b * nb + n, 0)
    prev = lambda b, n: (b * nb + jnp.maximum(n - 1, 0), 0)
    return pl.pallas_call(
        functools.partial(_swa_prompt_kernel, scale=scale),
        grid=(batch, nb),
        in_specs=[pl.BlockSpec(memory_space=pltpu.SMEM),
                  pl.BlockSpec((WINDOW, nq), cur),
                  pl.BlockSpec((WINDOW, nk), prev), pl.BlockSpec((WINDOW, nk), cur),
                  pl.BlockSpec((WINDOW, nk), prev), pl.BlockSpec((WINDOW, nk), cur),
                  pl.BlockSpec((1,) + bias2.shape[1:], lambda b, n: (jnp.minimum(n, 1), 0, 0, 0))],
        out_specs=pl.BlockSpec((WINDOW, nq), cur),
        out_shape=jax.ShapeDtypeStruct((m, nq), BF16),
        compiler_params=_params(("parallel", "arbitrary")),
        name="swa_prompt",
    )(sinks, q, k, k, v, v, bias2)


def _swa_sample_kernel(sink_ref, q_ref, kbt_ref, kn_ref, vbt_ref, vn_ref, bias_ref,
                       o_ref, kot_ref, vot_ref, *, bblk, t_new, nbuf, scale):
    hd = SWA_HEAD_DIM
    nk = kn_ref.shape[1]
    kvhs = range(SWA_KV_HEADS)
    sinks = [jnp.concatenate([jnp.full((t_new, 1), sink_ref[kvh * SWA_GROUP + g] * LOG2E, F32)
                              for g in range(SWA_GROUP)], axis=0) for kvh in kvhs]
    lane = lax.broadcasted_iota(jnp.int32, (nk, nbuf), 1)

    def shift_in(buf_t, new):
        placed = jnp.concatenate([jnp.zeros((nbuf - t_new, nk), F32), new], axis=0)
        placed_t = jnp.concatenate([placed[:, c * nbuf:(c + 1) * nbuf].T for c in range(nk // nbuf)], axis=0)
        return jnp.where(lane >= nbuf - t_new, placed_t, pltpu.roll(buf_t, nbuf - t_new, 1))

    def one(bi, carry):
        row0 = pl.multiple_of(bi * t_new, t_new)
        q = q_ref[pl.ds(row0, t_new), :]
        kn = kn_ref[pl.ds(row0, t_new), :]
        vn = vn_ref[pl.ds(row0, t_new), :]
        kbt = kbt_ref[bi]
        vbt = vbt_ref[bi]
        kot_ref[bi] = shift_in(kbt, kn)
        vot_ref[bi] = shift_in(vbt, vn)
        kb16, vb16 = kbt.astype(BF16), vbt.astype(BF16)
        zpad = jnp.zeros((t_new, nk), F32)
        kn16 = jnp.concatenate([kn, zpad], axis=0).astype(BF16)
        vn16 = jnp.concatenate([vn, zpad], axis=0).astype(BF16)
        qs = [jnp.concatenate([q[:, (kvh * SWA_GROUP + g) * hd:(kvh * SWA_GROUP + g + 1) * hd]
                               for g in range(SWA_GROUP)], axis=0) for kvh in kvhs]
        sb = [_dot(qs[kvh], kb16[kvh * hd:(kvh + 1) * hd, :]) for kvh in kvhs]
        sn = [_dot_nt(qs[kvh], kn16[:, kvh * hd:(kvh + 1) * hd]) for kvh in kvhs]
        eb, en, rs = [], [], []
        for kvh in kvhs:
            s_b = sb[kvh] * (scale * LOG2E) + bias_ref[kvh, :, :nbuf]
            s_n = sn[kvh] * (scale * LOG2E) + bias_ref[kvh, :, nbuf:]
            m = jnp.maximum(jnp.maximum(jnp.max(s_b, axis=-1, keepdims=True), jnp.max(s_n, axis=-1, keepdims=True)),
                            sinks[kvh])
            e_b, e_n = jnp.exp2(s_b - m), jnp.exp2(s_n - m)
            rs.append(1.0 / (jnp.sum(e_b, axis=-1, keepdims=True) + jnp.sum(e_n, axis=-1, keepdims=True)
                             + jnp.exp2(sinks[kvh] - m)))
            eb.append(e_b.astype(BF16))
            en.append(e_n.astype(BF16))
        outs = []
        for kvh in kvhs:
            o = (_dot_nt(eb[kvh], vb16[kvh * hd:(kvh + 1) * hd, :])
                 + _dot(en[kvh], vn16[:, kvh * hd:(kvh + 1) * hd])) * rs[kvh]
            outs += [o[g * t_new:(g + 1) * t_new] for g in range(SWA_GROUP)]
        o_ref[pl.ds(row0, t_new), :] = jnp.concatenate(outs, axis=1).astype(o_ref.dtype)
        return carry

    lax.fori_loop(0, bblk, one, 0)


def _swa_sample(q, k, v, buf_kt, buf_vt, bias, sinks, t_new, bblk):
    bd, nk, nbuf = buf_kt.shape
    assert bd % bblk == 0 and nbuf == LANES and nk % nbuf == 0 and bias.shape[2] == nbuf + 2 * t_new
    nq = q.shape[1]
    scale = SWA_HEAD_DIM ** -0.5
    rows = lambda i: (i, 0)
    bufs = lambda i: (i, 0, 0)
    return pl.pallas_call(
        functools.partial(_swa_sample_kernel, bblk=bblk, t_new=t_new, nbuf=nbuf, scale=scale),
        grid=(bd // bblk,),
        in_specs=[pl.BlockSpec(memory_space=pltpu.SMEM),
                  pl.BlockSpec((bblk * t_new, nq), rows),
                  pl.BlockSpec((bblk, nk, nbuf), bufs), pl.BlockSpec((bblk * t_new, nk), rows),
                  pl.BlockSpec((bblk, nk, nbuf), bufs), pl.BlockSpec((bblk * t_new, nk), rows),
                  _full(bias.shape)],
        out_specs=[pl.BlockSpec((bblk * t_new, nq), rows),
                   pl.BlockSpec((bblk, nk, nbuf), bufs), pl.BlockSpec((bblk, nk, nbuf), bufs)],
        out_shape=[jax.ShapeDtypeStruct((bd * t_new, nq), BF16),
                   jax.ShapeDtypeStruct((bd, nk, nbuf), F32), jax.ShapeDtypeStruct((bd, nk, nbuf), F32)],
        compiler_params=_params(("parallel",)),
        name="swa_sample",
    )(sinks, q, buf_kt, k, buf_vt, v, bias)


def _mlp_kernel(x_ref, g_ref, w1_ref, w2_ref, o_ref, h_ref):
    f = pl.program_id(1)

    @pl.when(f == 0)
    def _():
        x = x_ref[...]
        h_ref[...] = _rms(x, g_ref[...]).astype(BF16)
        o_ref[...] = x

    a = jnp.maximum(_dot(h_ref[...], w1_ref[...]), 0.0)
    o_ref[...] += _dot((a * a).astype(BF16), w2_ref[...])


def _mlp(x, g, w1, w2, layer, tm, tf):
    m, d = x.shape
    ff = w1.shape[2]
    return pl.pallas_call(
        _mlp_kernel, grid=(m // tm, ff // tf),
        in_specs=[pl.BlockSpec((tm, d), lambda i, f: (i, 0)), pl.BlockSpec((1, d), lambda i, f: (0, 0)),
                  pl.BlockSpec((None, d, tf), lambda i, f: (layer, 0, f)),
                  pl.BlockSpec((None, tf, d), lambda i, f: (layer, f, 0))],
        out_specs=pl.BlockSpec((tm, d), lambda i, f: (i, 0)),
        out_shape=jax.ShapeDtypeStruct((m, d), F32),
        scratch_shapes=[pltpu.VMEM((tm, d), BF16)],
        compiler_params=_params(("parallel", "arbitrary")),
        name="mlp",
    )(x, g, w1, w2)


def _ple_kernel(x_ref, g_ref, wg_ref, p_ref, wp_ref, *rest, final):
    if final:
        gf_ref, o_ref = rest
    else:
        (o_ref,) = rest
    x = x_ref[...]
    gate = jax.nn.sigmoid(_dot(_rms(x, g_ref[...]).astype(BF16), wg_ref[...]))
    y = x + gate * _dot(p_ref[...].astype(BF16), wp_ref[...])
    if final:
        y = _rms(y, gf_ref[...])
    o_ref[...] = y


def _ple(x, g, w_gate, p, layer, w_proj, tm, g_final=None):
    m, d = x.shape
    pd = p.shape[2]
    row = lambda i: (i, 0)
    final = g_final is not None
    in_specs = [pl.BlockSpec((tm, d), row), _full(g.shape), _full(w_gate.shape),
                pl.BlockSpec((None, tm, pd), lambda i: (layer, i, 0)), _full(w_proj.shape)]
    args = [x, g, w_gate, p, w_proj]
    if final:
        in_specs.append(_full(g_final.shape))
        args.append(g_final)
    return pl.pallas_call(
        functools.partial(_ple_kernel, final=final), grid=(m // tm,),
        in_specs=in_specs, out_specs=pl.BlockSpec((tm, d), row),
        out_shape=jax.ShapeDtypeStruct((m, d), F32),
        compiler_params=_params(("parallel",)),
        name="ple_final" if final else "ple",
    )(*args)


def _rope_tables(pos):
    half = ROPE_DIM // 2
    inv = ROPE_BASE ** (-jnp.arange(half, dtype=F32) / half)
    ang = pos.astype(F32)[:, None] * inv[None, :]
    reps = LANES // half
    return jnp.tile(jnp.cos(ang), (1, reps)), jnp.tile(jnp.sin(ang), (1, reps))


def _rotate_half_cols(w):
    half = ROPE_DIM // 2
    return jnp.concatenate([-w[..., half:], w[..., :half]], axis=-1)


def _prep_mla_weights(w_in, w_qb, w_kvb, q_lora, kv_lora):
    d = w_in.shape[0]
    o = q_lora + kv_lora
    w_r = w_in[:, o:o + ROPE_DIM]
    zpad = jnp.zeros((d, LANES - ROPE_DIM), w_in.dtype)
    w_ext = jnp.concatenate([w_in[:, :o], w_r, zpad, _rotate_half_cols(w_r), zpad], axis=1).astype(BF16)
    wq = w_qb.reshape(q_lora, MLA_HEADS, NOPE_DIM + ROPE_DIM)
    zq = jnp.zeros((q_lora, MLA_HEADS, MLA_QK_PAD - NOPE_DIM - ROPE_DIM), w_qb.dtype)
    wq_pad = jnp.concatenate([wq, zq], axis=2).reshape(q_lora, MLA_HEADS * MLA_QK_PAD).astype(BF16)
    wrot = jnp.concatenate([_rotate_half_cols(wq[..., NOPE_DIM:]), zq], axis=2)
    wrot = wrot.reshape(q_lora, MLA_HEADS * LANES).astype(BF16)
    wkv = w_kvb.reshape(kv_lora, MLA_HEADS, NOPE_DIM + V_DIM)
    wuk_t = jnp.transpose(wkv[..., :NOPE_DIM], (1, 2, 0)).astype(BF16)
    wuv = jnp.transpose(wkv[..., NOPE_DIM:], (1, 0, 2)).astype(BF16)
    return w_ext, wq_pad, wrot, wuk_t, wuv


def _swa_bias_tables(rel_bias, t_new, nbuf, kpad):
    w = WINDOW
    qi = np.arange(w)
    kj = np.arange(2 * w)
    dist = (w + qi[:, None]) - kj[None, :]
    ok = (dist >= 0) & (dist < w)
    mask1 = np.where(ok, 0.0, NEG_INF).astype(np.float32)
    mask0 = np.where(ok & (kj[None, :] >= w), 0.0, NEG_INF).astype(np.float32)
    bucket = _t5_bucket_np(dist)
    bias2 = jnp.stack([_rel_bias(rel_bias, jnp.asarray(bucket), jnp.asarray(mask0)),
                       _rel_bias(rel_bias, jnp.asarray(bucket), jnp.asarray(mask1))])
    kj = np.arange(kpad)
    dist_s = (nbuf + np.arange(t_new))[:, None] - kj[None, :]
    ok_s = (dist_s >= 0) & (dist_s < w) & (kj[None, :] < nbuf + t_new)
    bias_s = _rel_bias(rel_bias, jnp.asarray(_t5_bucket_np(dist_s)),
                       jnp.asarray(np.where(ok_s, 0.0, NEG_INF).astype(np.float32)))
    bias_s = bias_s.reshape(SWA_KV_HEADS, SWA_GROUP * t_new, kpad)
    return bias2, bias_s


def kernel(x_prompt, x_sample, cache_mla_latent, cache_mla_rope, cache_swa_k, cache_swa_v, page_table, p_prompt, p_sample, norm_mix, norm_mlp, norm_ple, norm_final, mla_w_in, mla_g_q, mla_g_kv, mla_w_qb, mla_w_kvb, mla_w_o, swa_w_qkv, swa_b_qkv, swa_sinks, swa_w_o, rel_bias, mlp_w1, mlp_w2, ple_w_gate, ple_w_proj):
    batch, seq, d = x_prompt.shape
    bd, t_new, _ = x_sample.shape
    depth = norm_mix.shape[0]
    q_lora = mla_g_q.shape[1]
    kv_lora = mla_g_kv.shape[1]
    past = page_table.shape[1] * PAGE_SIZE
    nbuf = cache_swa_k.shape[2]
    mp, ms = batch * seq, bd * t_new
    mla_scale = (NOPE_DIM + ROPE_DIM) ** -0.5
    tmp = _row_tile(mp, 512)
    tms = _row_tile(ms, 512)

    xp = x_prompt.reshape(mp, d)
    xs = x_sample.reshape(ms, d)
    row2 = lambda a: a.reshape(1, -1)

    cos_p, sin_p = _rope_tables(jnp.arange(seq))
    cos_s, sin_s = _rope_tables(past + jnp.arange(t_new))
    cos_s, sin_s = jnp.tile(cos_s, (tms // t_new, 1)), jnp.tile(sin_s, (tms // t_new, 1))

    bias2, bias_s = _swa_bias_tables(rel_bias, t_new, nbuf, nbuf + 2 * t_new)

    w1_all, w2_all = mlp_w1.astype(BF16), mlp_w2.astype(BF16)
    outs = {k: [] for k in ("lat_p", "rope_p", "lat_s", "rope_s", "kp", "vp", "ks", "vs")}
    for i in range(depth):
        j = i // 2
        g_mix = row2(norm_mix[i])
        if i % 2 == 0:
            w_ext, wq_pad, wrot, wuk_t, wuv = _prep_mla_weights(mla_w_in[j], mla_w_qb[j], mla_w_kvb[j],
                                                                q_lora, kv_lora)
            gq, gkv = row2(mla_g_q[j]), row2(mla_g_kv[j])
            w_o = mla_w_o[j].astype(BF16)
            cq, ckv, ckvb, kr, krp = _mla_in(xp, g_mix, w_ext, gq, gkv, cos_p, sin_p, tmp)
            q = _mla_q(cq, wq_pad, wrot, cos_p, sin_p, tmp, mla_scale * math.log2(math.e))[0]
            kv = _matmul(ckvb, mla_w_kvb[j].astype(BF16), tmp, BF16)
            o = _mla_flash(q, kv, krp, batch, seq, _row_tile(seq, 4096), _row_tile(seq, 512))
            xp = _matmul_res(xp, o, w_o, tmp)
            outs["lat_p"].append(ckv.reshape(batch, seq, kv_lora))
            outs["rope_p"].append(kr.reshape(batch, seq, ROPE_DIM))
            cq, ckv, ckvb, kr, krp = _mla_in(xs, g_mix, w_ext, gq, gkv, cos_s, sin_s, tms)
            q, qlat = _mla_q(cq, wq_pad, wrot, cos_s, sin_s, tms, 1.0, wuk_t)
            rows = t_new * MLA_HEADS
            q_rope = q.reshape(bd, rows, MLA_QK_PAD)[..., NOPE_DIM:NOPE_DIM + ROPE_DIM]
            knew = jnp.concatenate([ckvb.reshape(bd, t_new, kv_lora),
                                    krp[:, :ROPE_DIM].reshape(bd, t_new, ROPE_DIM)], axis=2)
            knew = jnp.pad(knew, ((0, 0), (0, rows - t_new), (0, 0)))
            o_lat = _mla_decode(page_table, qlat.reshape(bd, rows, kv_lora), q_rope, knew, cache_mla_latent,
                                jnp.swapaxes(cache_mla_rope, 2, 3), j, _row_tile(page_table.shape[1], 32), 8)
            o = _head_matmul(o_lat.reshape(ms, MLA_HEADS * kv_lora), wuv)
            xs = _matmul_res(xs, o, w_o, tms)
            outs["lat_s"].append(ckv.reshape(bd, t_new, kv_lora))
            outs["rope_s"].append(kr.reshape(bd, t_new, ROPE_DIM))
        else:
            w_qkv = swa_w_qkv[j].astype(BF16)
            b_qkv = row2(swa_b_qkv[j])
            w_o = swa_w_o[j].astype(BF16)
            sinks = swa_sinks[j]
            nk = SWA_KV_HEADS * SWA_HEAD_DIM
            q, k, v = _swa_qkv(xp, g_mix, w_qkv, b_qkv, tmp)
            o = _swa_prompt(q, k, v, bias2, sinks, batch, seq)
            xp = _matmul_res(xp, o, w_o, tmp)
            tail = lambda a: a.reshape(batch, seq, nk)[:, -WINDOW:].reshape(batch, WINDOW, SWA_KV_HEADS, SWA_HEAD_DIM)
            outs["kp"].append(tail(k))
            outs["vp"].append(tail(v))
            q, k, v = _swa_qkv(xs, g_mix, w_qkv, b_qkv, tms)
            feat_major = lambda c: jnp.swapaxes(c.reshape(bd, nbuf, nk), 1, 2)
            o, k_new, v_new = _swa_sample(q, k, v, feat_major(cache_swa_k[j]), feat_major(cache_swa_v[j]),
                                          bias_s, sinks, t_new, _row_tile(bd, 8))
            xs = _matmul_res(xs, o, w_o, tms)
            pos_major = lambda c: jnp.swapaxes(c, 1, 2).reshape(bd, nbuf, SWA_KV_HEADS, SWA_HEAD_DIM)
            outs["ks"].append(pos_major(k_new))
            outs["vs"].append(pos_major(v_new))
        wg, wp = ple_w_gate[i].astype(BF16), ple_w_proj[i].astype(BF16)
        g_mlp, g_ple = row2(norm_mlp[i]), row2(norm_ple[i])
        g_fin = row2(norm_final) if i == depth - 1 else None
        tf = _row_tile(w1_all.shape[2], 1024)
        xp = _mlp(xp, g_mlp, w1_all, w2_all, i, tmp, tf)
        xs = _mlp(xs, g_mlp, w1_all, w2_all, i, tms, tf)
        xp = _ple(xp, g_ple, wg, p_prompt.reshape(depth, mp, -1), i, wp, tmp, g_fin)
        xs = _ple(xs, g_ple, wg, p_sample.reshape(depth, ms, -1), i, wp, tms, g_fin)

    st = lambda k: jnp.stack(outs[k])
    return (xp.reshape(batch, seq, d), xs.reshape(bd, t_new, d),
            st("lat_p"), st("rope_p"), st("lat_s"), st("rope_s"),
            st("kp"), st("vp"), st("ks"), st("vs"))
```

```python
import functools
import math

import numpy as np
import jax
import jax.numpy as jnp
from jax import lax
from jax.experimental import pallas as pl
from jax.experimental.pallas import tpu as pltpu

F32 = jnp.float32
BF16 = jnp.bfloat16

MLA_HEADS = 16
NOPE_DIM = 128
ROPE_DIM = 64
V_DIM = 128
ROPE_BASE = 10000.0
SWA_HEADS = 32
SWA_KV_HEADS = 8
SWA_GROUP = SWA_HEADS // SWA_KV_HEADS
SWA_HEAD_DIM = 64
WINDOW = 128
REL_BUCKETS = 32
REL_MAX_DIST = 128
PAGE_SIZE = 128
EPS = 1e-6

LANES = 128
MLA_QK_PAD = 2 * LANES
VMEM_LIMIT = 56 * 1024 * 1024
NEG_INF = float("-inf")
LOG2E = math.log2(math.e)


def _params(sem):
    return pltpu.CompilerParams(dimension_semantics=sem, vmem_limit_bytes=VMEM_LIMIT)


def _rms(xf, g):
    return xf * lax.rsqrt(jnp.mean(xf * xf, axis=-1, keepdims=True) + EPS) * g


def _dot(a, b):
    return jnp.dot(a, b, preferred_element_type=F32)


def _dot_nt(a, b):
    return lax.dot_general(a, b, (((1,), (1,)), ((), ())), preferred_element_type=F32)


def _row_tile(m, want):
    t = min(m, want)
    assert m % t == 0
    return t


def _full(shape):
    return pl.BlockSpec(shape, lambda *_: (0,) * len(shape))


def _mla_in_kernel(x_ref, g_ref, w_ref, gq_ref, gkv_ref, cos_ref, sin_ref,
                   cq_ref, ckv_ref, ckvb_ref, kr_ref, krp_ref, *, q_lora, kv_lora):
    h = _rms(x_ref[...], g_ref[...]).astype(BF16)
    z = _dot(h, w_ref[...])
    cq_ref[...] = _rms(z[:, :q_lora], gq_ref[...]).astype(BF16)
    ckv = _rms(z[:, q_lora:q_lora + kv_lora], gkv_ref[...])
    ckv_ref[...] = ckv
    ckvb_ref[...] = ckv.astype(BF16)
    o = q_lora + kv_lora
    kr = z[:, o:o + LANES] * cos_ref[...] + z[:, o + LANES:o + 2 * LANES] * sin_ref[...]
    kr_ref[...] = kr[:, :ROPE_DIM]
    krp_ref[...] = kr.astype(BF16)


def _mla_in(x, g, w_ext, gq, gkv, cos, sin, tm):
    m, d = x.shape
    q_lora, kv_lora = gq.shape[1], gkv.shape[1]
    nrep = cos.shape[0] // tm
    row = lambda i: (i, 0)
    tab = lambda i: (i % nrep, 0)
    return pl.pallas_call(
        functools.partial(_mla_in_kernel, q_lora=q_lora, kv_lora=kv_lora),
        grid=(m // tm,),
        in_specs=[pl.BlockSpec((tm, d), row), _full(g.shape), _full(w_ext.shape),
                  _full(gq.shape), _full(gkv.shape),
                  pl.BlockSpec((tm, LANES), tab), pl.BlockSpec((tm, LANES), tab)],
        out_specs=[pl.BlockSpec((tm, q_lora), row), pl.BlockSpec((tm, kv_lora), row),
                   pl.BlockSpec((tm, kv_lora), row), pl.BlockSpec((tm, ROPE_DIM), row),
                   pl.BlockSpec((tm, LANES), row)],
        out_shape=[jax.ShapeDtypeStruct((m, q_lora), BF16), jax.ShapeDtypeStruct((m, kv_lora), F32),
                   jax.ShapeDtypeStruct((m, kv_lora), BF16), jax.ShapeDtypeStruct((m, ROPE_DIM), F32),
                   jax.ShapeDtypeStruct((m, LANES), BF16)],
        compiler_params=_params(("parallel",)),
        name="mla_in",
    )(x, g, w_ext, gq, gkv, cos, sin)


def _mla_q_kernel(cq_ref, wq_ref, wrot_ref, cos_ref, sin_ref, *rest, absorb, q_scale):
    if absorb:
        wuk_ref, q_ref, qlat_ref = rest
    else:
        (q_ref,) = rest
    cq = cq_ref[...]
    cos = cos_ref[...] * q_scale
    sin = sin_ref[...] * q_scale
    kv_lora = wuk_ref.shape[2] if absorb else 0
    for h in range(MLA_HEADS):
        a = _dot(cq, wq_ref[:, h * MLA_QK_PAD:(h + 1) * MLA_QK_PAD])
        if h % 2 == 0:
            r2 = _dot(cq, wrot_ref[:, h * LANES:(h + 2) * LANES])
        r = r2[:, (h % 2) * LANES:(h % 2 + 1) * LANES]
        qn = a[:, :NOPE_DIM] * q_scale
        qr = a[:, NOPE_DIM:] * cos + r * sin
        if absorb:
            qn = qn.astype(BF16)
            q_ref[:, h * MLA_QK_PAD:h * MLA_QK_PAD + NOPE_DIM] = qn
            q_ref[:, h * MLA_QK_PAD + NOPE_DIM:(h + 1) * MLA_QK_PAD] = qr.astype(BF16)
            qlat_ref[:, h * kv_lora:(h + 1) * kv_lora] = _dot(qn, wuk_ref[h]).astype(BF16)
        else:
            q_ref[h * MLA_QK_PAD:h * MLA_QK_PAD + NOPE_DIM, :] = qn.T.astype(BF16)
            q_ref[h * MLA_QK_PAD + NOPE_DIM:(h + 1) * MLA_QK_PAD, :] = qr.T.astype(BF16)


def _mla_q(cq, wq_pad, wrot, cos, sin, tm, q_scale, wuk_t=None):
    m, q_lora = cq.shape
    nrep = cos.shape[0] // tm
    row = lambda i: (i, 0)
    tab = lambda i: (i % nrep, 0)
    absorb = wuk_t is not None
    in_specs = [pl.BlockSpec((tm, q_lora), row), _full(wq_pad.shape), _full(wrot.shape),
                pl.BlockSpec((tm, LANES), tab), pl.BlockSpec((tm, LANES), tab)]
    if absorb:
        out_specs = [pl.BlockSpec((tm, MLA_HEADS * MLA_QK_PAD), row)]
        out_shape = [jax.ShapeDtypeStruct((m, MLA_HEADS * MLA_QK_PAD), BF16)]
    else:
        out_specs = [pl.BlockSpec((MLA_HEADS * MLA_QK_PAD, tm), lambda i: (0, i))]
        out_shape = [jax.ShapeDtypeStruct((MLA_HEADS * MLA_QK_PAD, m), BF16)]
    args = [cq, wq_pad, wrot, cos, sin]
    if absorb:
        kv_lora = wuk_t.shape[2]
        in_specs.append(_full(wuk_t.shape))
        out_specs.append(pl.BlockSpec((tm, MLA_HEADS * kv_lora), row))
        out_shape.append(jax.ShapeDtypeStruct((m, MLA_HEADS * kv_lora), BF16))
        args.append(wuk_t)
    return pl.pallas_call(
        functools.partial(_mla_q_kernel, absorb=absorb, q_scale=q_scale),
        grid=(m // tm,), in_specs=in_specs, out_specs=out_specs, out_shape=out_shape,
        compiler_params=_params(("parallel",)),
        name="mla_q_absorb" if absorb else "mla_q",
    )(*args)


def _mla_kv_kernel(c_ref, w_ref, kn_ref, vt_ref):
    c = c_ref[...]
    for h in range(MLA_HEADS):
        kv = _dot(c, w_ref[:, h * (NOPE_DIM + V_DIM):(h + 1) * (NOPE_DIM + V_DIM)])
        kn_ref[:, h * NOPE_DIM:(h + 1) * NOPE_DIM] = kv[:, :NOPE_DIM].astype(BF16)
        vt_ref[h * V_DIM:(h + 1) * V_DIM, :] = kv[:, NOPE_DIM:].T.astype(BF16)


def _mla_kv(ckv, w_kvb, tm):
    m, k = ckv.shape
    return pl.pallas_call(
        _mla_kv_kernel, grid=(m // tm,),
        in_specs=[pl.BlockSpec((tm, k), lambda i: (i, 0)), _full(w_kvb.shape)],
        out_specs=[pl.BlockSpec((tm, MLA_HEADS * NOPE_DIM), lambda i: (i, 0)),
                   pl.BlockSpec((MLA_HEADS * V_DIM, tm), lambda i: (0, i))],
        out_shape=[jax.ShapeDtypeStruct((m, MLA_HEADS * NOPE_DIM), BF16),
                   jax.ShapeDtypeStruct((MLA_HEADS * V_DIM, m), BF16)],
        compiler_params=_params(("parallel",)),
        name="mla_kv",
    )(ckv, w_kvb)


def _matmul_res_kernel(x_ref, a_ref, w_ref, o_ref):
    o_ref[...] = x_ref[...] + _dot(a_ref[...], w_ref[...])


def _matmul_res(x, a, w, tm):
    m, k = a.shape
    n = w.shape[1]
    return pl.pallas_call(
        _matmul_res_kernel, grid=(m // tm,),
        in_specs=[pl.BlockSpec((tm, n), lambda i: (i, 0)), pl.BlockSpec((tm, k), lambda i: (i, 0)),
                  _full(w.shape)],
        out_specs=pl.BlockSpec((tm, n), lambda i: (i, 0)),
        out_shape=jax.ShapeDtypeStruct((m, n), F32),
        compiler_params=_params(("parallel",)),
        name="matmul_res",
    )(x, a, w)


def _head_matmul_kernel(a_ref, w_ref, o_ref):
    o_ref[...] = _dot(a_ref[...], w_ref[0]).astype(o_ref.dtype)


def _head_matmul(a, w):
    hh, k, n = w.shape
    m = a.shape[0]
    assert a.shape[1] == hh * k
    return pl.pallas_call(
        _head_matmul_kernel, grid=(hh,),
        in_specs=[pl.BlockSpec((m, k), lambda h: (0, h)), pl.BlockSpec((1, k, n), lambda h: (h, 0, 0))],
        out_specs=pl.BlockSpec((m, n), lambda h: (0, h)),
        out_shape=jax.ShapeDtypeStruct((m, hh * n), BF16),
        compiler_params=_params(("parallel",)),
        name="head_matmul",
    )(a, w)


def _mla_flash_kernel(qt_ref, kn_ref, kr_ref, vt_ref, o_ref, *, tk):
    seq = qt_ref.shape[1]
    qt = qt_ref[...]

    def update(c0, st, carry):
        m, l, acc = carry
        m_new = jnp.maximum(m, jnp.max(st, axis=0, keepdims=True))
        alpha = jnp.exp2(m - m_new)
        pt = jnp.exp2(st - m_new)
        l = alpha * l + jnp.sum(pt, axis=0, keepdims=True)
        acc = alpha * acc + _dot(vt_ref[:, c0:c0 + tk], pt.astype(BF16))
        return m_new, l, acc

    carry = (jnp.full((1, seq), NEG_INF, F32), jnp.zeros((1, seq), F32), jnp.zeros((V_DIM, seq), F32))
    for c0 in range(0, seq, tk):
        k = jnp.concatenate([kn_ref[c0:c0 + tk, :], kr_ref[c0:c0 + tk, :]], axis=1)
        key = lax.broadcasted_iota(jnp.int32, (tk, seq - c0), 0)
        qry = lax.broadcasted_iota(jnp.int32, (tk, seq - c0), 1)
        st = jnp.where(key <= qry, _dot(k, qt[:, c0:]), NEG_INF)
        sub = update(c0, st, tuple(x[:, c0:] for x in carry))
        carry = tuple(jnp.concatenate([x[:, :c0], y], axis=1) if c0 else y for x, y in zip(carry, sub))
    m, l, acc = carry
    o_ref[...] = (acc / l).T.astype(o_ref.dtype)


def _mla_flash(qt, kn, krp, vt, batch, seq, tk):
    m = kn.shape[0]
    assert seq % tk == 0 and tk % LANES == 0
    return pl.pallas_call(
        functools.partial(_mla_flash_kernel, tk=tk),
        grid=(batch, MLA_HEADS),
        in_specs=[pl.BlockSpec((MLA_QK_PAD, seq), lambda b, h: (h, b)),
                  pl.BlockSpec((seq, NOPE_DIM), lambda b, h: (b, h)),
                  pl.BlockSpec((seq, LANES), lambda b, h: (b, 0)),
                  pl.BlockSpec((V_DIM, seq), lambda b, h: (h, b))],
        out_specs=pl.BlockSpec((seq, V_DIM), lambda b, h: (b, h)),
        out_shape=jax.ShapeDtypeStruct((m, MLA_HEADS * V_DIM), BF16),
        compiler_params=_params(("parallel", "parallel")),
        name="mla_flash",
    )(qt, kn, krp, vt)


def _col(row):
    n = row.shape[1]
    return jnp.broadcast_to(row, (n, n)).T


def _scale_rows(x, col):
    n = col.shape[0]
    return jnp.concatenate([x[:, c * n:(c + 1) * n] * col for c in range(x.shape[1] // n)], axis=1)


def _mla_decode_kernel(pt_ref, ql_ref, qr_ref, kn_ref, lat_hbm, ropet_hbm, o_ref,
                       lat_buf, rope_buf, sem, m_ref, l_ref, acc_ref,
                       *, layer, n_pg, n_sub, n_chunk, t_new, scale):
    b = pl.program_id(0)
    nb = pl.num_programs(0)
    c_dim = lat_buf.shape[2]

    def page_copies(bb, ch, slot):
        cps = []
        for g in range(n_pg):
            page = pt_ref[bb, ch * n_pg + g]
            cps.append(pltpu.make_async_copy(lat_hbm.at[layer, page],
                                             lat_buf.at[slot, pl.ds(g * PAGE_SIZE, PAGE_SIZE)], sem.at[0, slot]))
            cps.append(pltpu.make_async_copy(ropet_hbm.at[layer, page], rope_buf.at[slot, g], sem.at[1, slot]))
        return cps

    @pl.when(b == 0)
    def _():
        for cp in page_copies(0, 0, 0):
            cp.start()

    m_ref[...] = jnp.full(m_ref.shape, NEG_INF, F32)
    l_ref[...] = jnp.zeros(l_ref.shape, F32)
    acc_ref[...] = jnp.zeros(acc_ref.shape, F32)
    qt_lat = ql_ref[0].astype(F32).T.astype(BF16)
    qt_rope = qr_ref[0].astype(F32).T.astype(BF16)

    def attend(kc, st):
        m_old = m_ref[...]
        m_new = jnp.maximum(m_old, jnp.max(st, axis=0, keepdims=True))
        alpha = jnp.exp(m_old - m_new)
        pt = jnp.exp(st - m_new)
        l_ref[...] = alpha * l_ref[...] + jnp.sum(pt, axis=0, keepdims=True)
        m_ref[...] = m_new
        pv = _dot(pt.T.astype(BF16), kc)
        acc_ref[...] = _scale_rows(acc_ref[...], _col(alpha)) + pv

    pg_sub = n_pg // n_sub

    def chunk(ch, carry):
        slot = ch % 2
        last = ch + 1 == n_chunk
        nxt_b = jnp.where(last, b + 1, b)
        nxt_ch = jnp.where(last, 0, ch + 1)

        @pl.when(nxt_b < nb)
        def _():
            for cp in page_copies(nxt_b, nxt_ch, 1 - slot):
                cp.start()

        for cp in page_copies(b, ch, slot):
            cp.wait()
        kcs, sts = [], []
        for u in range(n_sub):
            kc = lat_buf[slot, u * pg_sub * PAGE_SIZE:(u + 1) * pg_sub * PAGE_SIZE, :].astype(BF16)
            kr = jnp.concatenate([rope_buf[slot, u * pg_sub + g].T for g in range(pg_sub)], axis=0).astype(BF16)
            kcs.append(kc)
            sts.append((_dot(kc, qt_lat) + _dot(kr, qt_rope)) * scale)
        for u in range(n_sub):
            attend(kcs[u], sts[u])
        return carry

    lax.fori_loop(0, n_chunk, chunk, 0)

    kn = kn_ref[0]
    sn = (_dot(kn[:, :c_dim], qt_lat) + _dot(kn[:, c_dim:], qt_rope)) * scale
    u = lax.broadcasted_iota(jnp.int32, sn.shape, 0)
    t = lax.broadcasted_iota(jnp.int32, sn.shape, 1) // (sn.shape[1] // t_new)
    attend(kn[:, :c_dim], jnp.where(u <= t, sn, NEG_INF))
    o_ref[0] = _scale_rows(acc_ref[...], _col(1.0 / l_ref[...])).astype(o_ref.dtype)


def _mla_decode(page_table, qlat, qrope, knew, lat_pool, ropet_pool, layer, n_pg, n_sub):
    bd, rows, c_dim = qlat.shape
    r_dim = ropet_pool.shape[2]
    cr = c_dim + r_dim
    assert lat_pool.shape[3] == c_dim and qrope.shape == (bd, rows, r_dim)
    assert knew.shape == (bd, rows, cr) and rows == LANES
    t_new = rows // MLA_HEADS
    n_pages = page_table.shape[1]
    n_chunk = n_pages // n_pg
    assert n_pages == n_chunk * n_pg and n_chunk % 2 == 0 and n_pg % n_sub == 0
    scale = (NOPE_DIM + ROPE_DIM) ** -0.5
    per_b = lambda b, pt: (b, 0, 0)
    grid_spec = pltpu.PrefetchScalarGridSpec(
        num_scalar_prefetch=1, grid=(bd,),
        in_specs=[pl.BlockSpec((1, rows, c_dim), per_b), pl.BlockSpec((1, rows, r_dim), per_b),
                  pl.BlockSpec((1, rows, cr), per_b),
                  pl.BlockSpec(memory_space=pl.ANY), pl.BlockSpec(memory_space=pl.ANY)],
        out_specs=pl.BlockSpec((1, rows, c_dim), per_b),
        scratch_shapes=[pltpu.VMEM((2, n_pg * PAGE_SIZE, c_dim), F32),
                        pltpu.VMEM((2, n_pg, r_dim, PAGE_SIZE), F32),
                        pltpu.SemaphoreType.DMA((2, 2)),
                        pltpu.VMEM((1, rows), F32), pltpu.VMEM((1, rows), F32), pltpu.VMEM((rows, c_dim), F32)])
    return pl.pallas_call(
        functools.partial(_mla_decode_kernel, layer=layer, n_pg=n_pg, n_sub=n_sub, n_chunk=n_chunk, t_new=t_new,
                          scale=scale),
        grid_spec=grid_spec,
        out_shape=jax.ShapeDtypeStruct((bd, rows, c_dim), BF16),
        compiler_params=_params(("arbitrary",)),
        name="mla_decode",
    )(page_table, qlat, qrope, knew, lat_pool, ropet_pool)


def _t5_bucket_np(dist):
    n = np.maximum(dist, 0)
    exact = REL_BUCKETS // 2
    nf = np.maximum(n, 1).astype(np.float32)
    large = exact + (np.log(nf / np.float32(exact)) / np.float32(math.log(REL_MAX_DIST / exact))
                     * np.float32(REL_BUCKETS - exact)).astype(np.int32)
    large = np.minimum(large, REL_BUCKETS - 1)
    return np.where(n < exact, n, large).astype(np.int32)


def _bias_kernel(tab_ref, bucket_ref, mask_ref, o_ref):
    h = pl.program_id(0)
    bucket = bucket_ref[...]
    acc = mask_ref[...]
    for b in range(REL_BUCKETS):
        acc = acc + jnp.where(bucket == b, tab_ref[b, h], 0.0)
    o_ref[0] = acc * LOG2E


def _rel_bias(rel_table, bucket, mask_add):
    rr, cc = bucket.shape
    return pl.pallas_call(
        _bias_kernel, grid=(SWA_HEADS,),
        in_specs=[pl.BlockSpec(memory_space=pltpu.SMEM), _full((rr, cc)), _full((rr, cc))],
        out_specs=pl.BlockSpec((1, rr, cc), lambda h: (h, 0, 0)),
        out_shape=jax.ShapeDtypeStruct((SWA_HEADS, rr, cc), F32),
        compiler_params=_params(("parallel",)),
        name="rel_bias",
    )(rel_table, bucket, mask_add)


def _rms_matmul_bias_kernel(x_ref, g_ref, w_ref, b_ref, q_ref, k_ref, v_ref, *, nq, nk):
    h = _rms(x_ref[...], g_ref[...]).astype(BF16)
    z = _dot(h, w_ref[...]) + b_ref[...]
    q_ref[...] = z[:, :nq].astype(BF16)
    k_ref[...] = z[:, nq:nq + nk]
    v_ref[...] = z[:, nq + nk:]


def _swa_qkv(x, g, w, b, tm):
    m, d = x.shape
    nq = SWA_HEADS * SWA_HEAD_DIM
    nk = SWA_KV_HEADS * SWA_HEAD_DIM
    row = lambda i: (i, 0)
    return pl.pallas_call(
        functools.partial(_rms_matmul_bias_kernel, nq=nq, nk=nk),
        grid=(m // tm,),
        in_specs=[pl.BlockSpec((tm, d), row), _full(g.shape), _full(w.shape), _full(b.shape)],
        out_specs=[pl.BlockSpec((tm, nq), row), pl.BlockSpec((tm, nk), row), pl.BlockSpec((tm, nk), row)],
        out_shape=[jax.ShapeDtypeStruct((m, nq), BF16), jax.ShapeDtypeStruct((m, nk), F32),
                   jax.ShapeDtypeStruct((m, nk), F32)],
        compiler_params=_params(("parallel",)),
        name="swa_qkv",
    )(x, g, w, b)


def _swa_prompt_kernel(sink_ref, q_ref, kp_ref, kc_ref, vp_ref, vc_ref, bias_ref, o_ref, *, scale):
    q = q_ref[...]
    kk = jnp.concatenate([kp_ref[...], kc_ref[...]], axis=0).astype(BF16)
    vv = jnp.concatenate([vp_ref[...], vc_ref[...]], axis=0).astype(BF16)
    hd = SWA_HEAD_DIM
    w = q.shape[0]
    kvhs = range(SWA_KV_HEADS)
    qs = [jnp.concatenate([q[:, (kvh * SWA_GROUP + g) * hd:(kvh * SWA_GROUP + g + 1) * hd] for g in range(SWA_GROUP)],
                          axis=0) for kvh in kvhs]
    ss = [_dot_nt(qs[kvh], kk[:, kvh * hd:(kvh + 1) * hd]) for kvh in kvhs]
    ones = jnp.ones((2 * w, LANES), BF16)
    es, sink_terms = [], []
    for kvh in kvhs:
        sink = jnp.concatenate([jnp.full((w, LANES), sink_ref[kvh * SWA_GROUP + g] * LOG2E, F32)
                                for g in range(SWA_GROUP)], axis=0)
        s = ss[kvh] * (scale * LOG2E) + bias_ref[0, kvh]
        m = jnp.maximum(jnp.broadcast_to(jnp.max(s, axis=-1, keepdims=True), sink.shape), sink)
        es.append(jnp.exp2(s - jnp.concatenate([m] * (2 * w // LANES), axis=1)).astype(BF16))
        sink_terms.append(jnp.exp2(sink - m))
    outs = []
    for kvh in kvhs:
        pair = kvh // 2 * 2 * hd
        oe = _dot(es[kvh], jnp.concatenate([vv[:, pair:pair + LANES], ones], axis=1))
        o = oe[:, :LANES] / (oe[:, LANES:] + sink_terms[kvh])
        half = kvh % 2 * hd
        outs += [o[g * w:(g + 1) * w, half:half + hd] for g in range(SWA_GROUP)]
    o_ref[...] = jnp.concatenate(outs, axis=1).astype(o_ref.dtype)


def _swa_prompt(q, k, v, bias2, sinks, batch, seq):
    bias2 = bias2.reshape(2, SWA_KV_HEADS, SWA_GROUP * WINDOW, 2 * WINDOW)
    m = q.shape[0]
    nb = seq // WINDOW
    nq = q.shape[1]
    nk = k.shape[1]
    scale = SWA_HEAD_DIM ** -0.5
    cur = lambda b, n: (b * nb + n, 0)
    prev = lambda b, n: (b * nb + jnp.maximum(n - 1, 0), 0)
    return pl.pallas_call(
        functools.partial(_swa_prompt_kernel, scale=scale),
        grid=(batch, nb),
        in_specs=[pl.BlockSpec(memory_space=pltpu.SMEM),
                  pl.BlockSpec((WINDOW, nq), cur),
                  pl.BlockSpec((WINDOW, nk), prev), pl.BlockSpec((WINDOW, nk), cur),
                  pl.BlockSpec((WINDOW, nk), prev), pl.BlockSpec((WINDOW, nk), cur),
                  pl.BlockSpec((1,) + bias2.shape[1:], lambda b, n: (jnp.minimum(n, 1), 0, 0, 0))],
        out_specs=pl.BlockSpec((WINDOW, nq), cur),
        out_shape=jax.ShapeDtypeStruct((m, nq), BF16),
        compiler_params=_params(("parallel", "arbitrary")),
        name="swa_prompt",
    )(sinks, q, k, k, v, v, bias2)


def _swa_sample_kernel(sink_ref, q_ref, kbt_ref, kn_ref, vbt_ref, vn_ref, bias_ref,
                       o_ref, kot_ref, vot_ref, *, bblk, t_new, nbuf, scale):
    hd = SWA_HEAD_DIM
    nk = kn_ref.shape[1]
    kvhs = range(SWA_KV_HEADS)
    sinks = [jnp.concatenate([jnp.full((t_new, 1), sink_ref[kvh * SWA_GROUP + g] * LOG2E, F32)
                              for g in range(SWA_GROUP)], axis=0) for kvh in kvhs]
    lane = lax.broadcasted_iota(jnp.int32, (nk, nbuf), 1)

    def shift_in(buf_t, new):
        placed = jnp.concatenate([jnp.zeros((nbuf - t_new, nk), F32), new], axis=0)
        placed_t = jnp.concatenate([placed[:, c * nbuf:(c + 1) * nbuf].T for c in range(nk // nbuf)], axis=0)
        return jnp.where(lane >= nbuf - t_new, placed_t, pltpu.roll(buf_t, nbuf - t_new, 1))

    def one(bi, carry):
        row0 = pl.multiple_of(bi * t_new, t_new)
        q = q_ref[pl.ds(row0, t_new), :]
        kn = kn_ref[pl.ds(row0, t_new), :]
        vn = vn_ref[pl.ds(row0, t_new), :]
        kbt = kbt_ref[bi]
        vbt = vbt_ref[bi]
        kot_ref[bi] = shift_in(kbt, kn)
        vot_ref[bi] = shift_in(vbt, vn)
        kb16, vb16 = kbt.astype(BF16), vbt.astype(BF16)
        zpad = jnp.zeros((t_new, nk), F32)
        kn16 = jnp.concatenate([kn, zpad], axis=0).astype(BF16)
        vn16 = jnp.concatenate([vn, zpad], axis=0).astype(BF16)
        qs = [jnp.concatenate([q[:, (kvh * SWA_GROUP + g) * hd:(kvh * SWA_GROUP + g + 1) * hd]
                               for g in range(SWA_GROUP)], axis=0) for kvh in kvhs]
        sb = [_dot(qs[kvh], kb16[kvh * hd:(kvh + 1) * hd, :]) for kvh in kvhs]
        sn = [_dot_nt(qs[kvh], kn16[:, kvh * hd:(kvh + 1) * hd]) for kvh in kvhs]
        eb, en, rs = [], [], []
        for kvh in kvhs:
            s_b = sb[kvh] * (scale * LOG2E) + bias_ref[kvh, :, :nbuf]
            s_n = sn[kvh] * (scale * LOG2E) + bias_ref[kvh, :, nbuf:]
            m = jnp.maximum(jnp.maximum(jnp.max(s_b, axis=-1, keepdims=True), jnp.max(s_n, axis=-1, keepdims=True)),
                            sinks[kvh])
            e_b, e_n = jnp.exp2(s_b - m), jnp.exp2(s_n - m)
            rs.append(1.0 / (jnp.sum(e_b, axis=-1, keepdims=True) + jnp.sum(e_n, axis=-1, keepdims=True)
                             + jnp.exp2(sinks[kvh] - m)))
            eb.append(e_b.astype(BF16))
            en.append(e_n.astype(BF16))
        outs = []
        for kvh in kvhs:
            o = (_dot_nt(eb[kvh], vb16[kvh * hd:(kvh + 1) * hd, :])
                 + _dot(en[kvh], vn16[:, kvh * hd:(kvh + 1) * hd])) * rs[kvh]
            outs += [o[g * t_new:(g + 1) * t_new] for g in range(SWA_GROUP)]
        o_ref[pl.ds(row0, t_new), :] = jnp.concatenate(outs, axis=1).astype(o_ref.dtype)
        return carry

    lax.fori_loop(0, bblk, one, 0)


def _swa_sample(q, k, v, buf_kt, buf_vt, bias, sinks, t_new, bblk):
    bd, nk, nbuf = buf_kt.shape
    assert bd % bblk == 0 and nbuf == LANES and nk % nbuf == 0 and bias.shape[2] == nbuf + 2 * t_new
    nq = q.shape[1]
    scale = SWA_HEAD_DIM ** -0.5
    rows = lambda i: (i, 0)
    bufs = lambda i: (i, 0, 0)
    return pl.pallas_call(
        functools.partial(_swa_sample_kernel, bblk=bblk, t_new=t_new, nbuf=nbuf, scale=scale),
        grid=(bd // bblk,),
        in_specs=[pl.BlockSpec(memory_space=pltpu.SMEM),
                  pl.BlockSpec((bblk * t_new, nq), rows),
                  pl.BlockSpec((bblk, nk, nbuf), bufs), pl.BlockSpec((bblk * t_new, nk), rows),
                  pl.BlockSpec((bblk, nk, nbuf), bufs), pl.BlockSpec((bblk * t_new, nk), rows),
                  _full(bias.shape)],
        out_specs=[pl.BlockSpec((bblk * t_new, nq), rows),
                   pl.BlockSpec((bblk, nk, nbuf), bufs), pl.BlockSpec((bblk, nk, nbuf), bufs)],
        out_shape=[jax.ShapeDtypeStruct((bd * t_new, nq), BF16),
                   jax.ShapeDtypeStruct((bd, nk, nbuf), F32), jax.ShapeDtypeStruct((bd, nk, nbuf), F32)],
        compiler_params=_params(("parallel",)),
        name="swa_sample",
    )(sinks, q, buf_kt, k, buf_vt, v, bias)


def _mlp_kernel(x_ref, g_ref, w1_ref, w2_ref, o_ref, h_ref):
    f = pl.program_id(1)

    @pl.when(f == 0)
    def _():
        x = x_ref[...]
        h_ref[...] = _rms(x, g_ref[...]).astype(BF16)
        o_ref[...] = x

    a = jnp.maximum(_dot(h_ref[...], w1_ref[...]), 0.0)
    o_ref[...] += _dot((a * a).astype(BF16), w2_ref[...])


def _mlp(x, g, w1, w2, layer, tm, tf):
    m, d = x.shape
    ff = w1.shape[2]
    return pl.pallas_call(
        _mlp_kernel, grid=(m // tm, ff // tf),
        in_specs=[pl.BlockSpec((tm, d), lambda i, f: (i, 0)), pl.BlockSpec((1, d), lambda i, f: (0, 0)),
                  pl.BlockSpec((None, d, tf), lambda i, f: (layer, 0, f)),
                  pl.BlockSpec((None, tf, d), lambda i, f: (layer, f, 0))],
        out_specs=pl.BlockSpec((tm, d), lambda i, f: (i, 0)),
        out_shape=jax.ShapeDtypeStruct((m, d), F32),
        scratch_shapes=[pltpu.VMEM((tm, d), BF16)],
        compiler_params=_params(("parallel", "arbitrary")),
        name="mlp",
    )(x, g, w1, w2)


def _ple_kernel(x_ref, g_ref, wg_ref, p_ref, wp_ref, *rest, final):
    if final:
        gf_ref, o_ref = rest
    else:
        (o_ref,) = rest
    x = x_ref[...]
    gate = jax.nn.sigmoid(_dot(_rms(x, g_ref[...]).astype(BF16), wg_ref[...]))
    y = x + gate * _dot(p_ref[...].astype(BF16), wp_ref[...])
    if final:
        y = _rms(y, gf_ref[...])
    o_ref[...] = y


def _ple(x, g, w_gate, p, layer, w_proj, tm, g_final=None):
    m, d = x.shape
    pd = p.shape[2]
    row = lambda i: (i, 0)
    final = g_final is not None
    in_specs = [pl.BlockSpec((tm, d), row), _full(g.shape), _full(w_gate.shape),
                pl.BlockSpec((None, tm, pd), lambda i: (layer, i, 0)), _full(w_proj.shape)]
    args = [x, g, w_gate, p, w_proj]
    if final:
        in_specs.append(_full(g_final.shape))
        args.append(g_final)
    return pl.pallas_call(
        functools.partial(_ple_kernel, final=final), grid=(m // tm,),
        in_specs=in_specs, out_specs=pl.BlockSpec((tm, d), row),
        out_shape=jax.ShapeDtypeStruct((m, d), F32),
        compiler_params=_params(("parallel",)),
        name="ple_final" if final else "ple",
    )(*args)


def _rope_tables(pos):
    half = ROPE_DIM // 2
    inv = ROPE_BASE ** (-jnp.arange(half, dtype=F32) / half)
    ang = pos.astype(F32)[:, None] * inv[None, :]
    reps = LANES // half
    return jnp.tile(jnp.cos(ang), (1, reps)), jnp.tile(jnp.sin(ang), (1, reps))


def _rotate_half_cols(w):
    half = ROPE_DIM // 2
    return jnp.concatenate([-w[..., half:], w[..., :half]], axis=-1)


def _prep_mla_weights(w_in, w_qb, w_kvb, q_lora, kv_lora):
    d = w_in.shape[0]
    o = q_lora + kv_lora
    w_r = w_in[:, o:o + ROPE_DIM]
    zpad = jnp.zeros((d, LANES - ROPE_DIM), w_in.dtype)
    w_ext = jnp.concatenate([w_in[:, :o], w_r, zpad, _rotate_half_cols(w_r), zpad], axis=1).astype(BF16)
    wq = w_qb.reshape(q_lora, MLA_HEADS, NOPE_DIM + ROPE_DIM)
    zq = jnp.zeros((q_lora, MLA_HEADS, MLA_QK_PAD - NOPE_DIM - ROPE_DIM), w_qb.dtype)
    wq_pad = jnp.concatenate([wq, zq], axis=2).reshape(q_lora, MLA_HEADS * MLA_QK_PAD).astype(BF16)
    wrot = jnp.concatenate([_rotate_half_cols(wq[..., NOPE_DIM:]), zq], axis=2)
    wrot = wrot.reshape(q_lora, MLA_HEADS * LANES).astype(BF16)
    wkv = w_kvb.reshape(kv_lora, MLA_HEADS, NOPE_DIM + V_DIM)
    wuk_t = jnp.transpose(wkv[..., :NOPE_DIM], (1, 2, 0)).astype(BF16)
    wuv = jnp.transpose(wkv[..., NOPE_DIM:], (1, 0, 2)).astype(BF16)
    return w_ext, wq_pad, wrot, wuk_t, wuv


def _swa_bias_tables(rel_bias, t_new, nbuf, kpad):
    w = WINDOW
    qi = np.arange(w)
    kj = np.arange(2 * w)
    dist = (w + qi[:, None]) - kj[None, :]
    ok = (dist >= 0) & (dist < w)
    mask1 = np.where(ok, 0.0, NEG_INF).astype(np.float32)
    mask0 = np.where(ok & (kj[None, :] >= w), 0.0, NEG_INF).astype(np.float32)
    bucket = _t5_bucket_np(dist)
    bias2 = jnp.stack([_rel_bias(rel_bias, jnp.asarray(bucket), jnp.asarray(mask0)),
                       _rel_bias(rel_bias, jnp.asarray(bucket), jnp.asarray(mask1))])
    kj = np.arange(kpad)
    dist_s = (nbuf + np.arange(t_new))[:, None] - kj[None, :]
    ok_s = (dist_s >= 0) & (dist_s < w) & (kj[None, :] < nbuf + t_new)
    bias_s = _rel_bias(rel_bias, jnp.asarray(_t5_bucket_np(dist_s)),
                       jnp.asarray(np.where(ok_s, 0.0, NEG_INF).astype(np.float32)))
    bias_s = bias_s.reshape(SWA_KV_HEADS, SWA_GROUP * t_new, kpad)
    return bias2, bias_s


def kernel(x_prompt, x_sample, cache_mla_latent, cache_mla_rope, cache_swa_k, cache_swa_v, page_table, p_prompt, p_sample, norm_mix, norm_mlp, norm_ple, norm_final, mla_w_in, mla_g_q, mla_g_kv, mla_w_qb, mla_w_kvb, mla_w_o, swa_w_qkv, swa_b_qkv, swa_sinks, swa_w_o, rel_bias, mlp_w1, mlp_w2, ple_w_gate, ple_w_proj):
    batch, seq, d = x_prompt.shape
    bd, t_new, _ = x_sample.shape
    depth = norm_mix.shape[0]
    q_lora = mla_g_q.shape[1]
    kv_lora = mla_g_kv.shape[1]
    past = page_table.shape[1] * PAGE_SIZE
    nbuf = cache_swa_k.shape[2]
    mp, ms = batch * seq, bd * t_new
    mla_scale = (NOPE_DIM + ROPE_DIM) ** -0.5
    tmp = _row_tile(mp, 512)
    tms = _row_tile(ms, 512)

    xp = x_prompt.reshape(mp, d)
    xs = x_sample.reshape(ms, d)
    row2 = lambda a: a.reshape(1, -1)

    cos_p, sin_p = _rope_tables(jnp.arange(seq))
    cos_s, sin_s = _rope_tables(past + jnp.arange(t_new))
    cos_s, sin_s = jnp.tile(cos_s, (tms // t_new, 1)), jnp.tile(sin_s, (tms // t_new, 1))

    bias2, bias_s = _swa_bias_tables(rel_bias, t_new, nbuf, nbuf + 2 * t_new)

    w1_all, w2_all = mlp_w1.astype(BF16), mlp_w2.astype(BF16)
    outs = {k: [] for k in ("lat_p", "rope_p", "lat_s", "rope_s", "kp", "vp", "ks", "vs")}
    for i in range(depth):
        j = i // 2
        g_mix = row2(norm_mix[i])
        if i % 2 == 0:
            w_ext, wq_pad, wrot, wuk_t, wuv = _prep_mla_weights(mla_w_in[j], mla_w_qb[j], mla_w_kvb[j],
                                                                q_lora, kv_lora)
            gq, gkv = row2(mla_g_q[j]), row2(mla_g_kv[j])
            w_o = mla_w_o[j].astype(BF16)
            cq, ckv, ckvb, kr, krp = _mla_in(xp, g_mix, w_ext, gq, gkv, cos_p, sin_p, tmp)
            qt = _mla_q(cq, wq_pad, wrot, cos_p, sin_p, tmp, mla_scale * LOG2E)[0]
            kn, vt = _mla_kv(ckvb, mla_w_kvb[j].astype(BF16), tmp)
            o = _mla_flash(qt, kn, krp, vt, batch, seq, _row_tile(seq, 512))
            xp = _matmul_res(xp, o, w_o, tmp)
            outs["lat_p"].append(ckv.reshape(batch, seq, kv_lora))
            outs["rope_p"].append(kr.reshape(batch, seq, ROPE_DIM))
            cq, ckv, ckvb, kr, krp = _mla_in(xs, g_mix, w_ext, gq, gkv, cos_s, sin_s, tms)
            q, qlat = _mla_q(cq, wq_pad, wrot, cos_s, sin_s, tms, 1.0, wuk_t)
            rows = t_new * MLA_HEADS
            q_rope = q.reshape(bd, rows, MLA_QK_PAD)[..., NOPE_DIM:NOPE_DIM + ROPE_DIM]
            knew = jnp.concatenate([ckvb.reshape(bd, t_new, kv_lora),
                                    krp[:, :ROPE_DIM].reshape(bd, t_new, ROPE_DIM)], axis=2)
            knew = jnp.pad(knew, ((0, 0), (0, rows - t_new), (0, 0)))
            o_lat = _mla_decode(page_table, qlat.reshape(bd, rows, kv_lora), q_rope, knew, cache_mla_latent,
                                jnp.swapaxes(cache_mla_rope, 2, 3), j, _row_tile(page_table.shape[1], 32), 8)
            o = _head_matmul(o_lat.reshape(ms, MLA_HEADS * kv_lora), wuv)
            xs = _matmul_res(xs, o, w_o, tms)
            outs["lat_s"].append(ckv.reshape(bd, t_new, kv_lora))
            outs["rope_s"].append(kr.reshape(bd, t_new, ROPE_DIM))
        else:
            w_qkv = swa_w_qkv[j].astype(BF16)
            b_qkv = row2(swa_b_qkv[j])
            w_o = swa_w_o[j].astype(BF16)
            sinks = swa_sinks[j]
            nk = SWA_KV_HEADS * SWA_HEAD_DIM
            q, k, v = _swa_qkv(xp, g_mix, w_qkv, b_qkv, tmp)
            o = _swa_prompt(q, k, v, bias2, sinks, batch, seq)
            xp = _matmul_res(xp, o, w_o, tmp)
            tail = lambda a: a.reshape(batch, seq, nk)[:, -WINDOW:].reshape(batch, WINDOW, SWA_KV_HEADS, SWA_HEAD_DIM)
            outs["kp"].append(tail(k))
            outs["vp"].append(tail(v))
            q, k, v = _swa_qkv(xs, g_mix, w_qkv, b_qkv, tms)
            feat_major = lambda c: jnp.swapaxes(c.reshape(bd, nbuf, nk), 1, 2)
            o, k_new, v_new = _swa_sample(q, k, v, feat_major(cache_swa_k[j]), feat_major(cache_swa_v[j]),
                                          bias_s, sinks, t_new, _row_tile(bd, 8))
            xs = _matmul_res(xs, o, w_o, tms)
            pos_major = lambda c: jnp.swapaxes(c, 1, 2).reshape(bd, nbuf, SWA_KV_HEADS, SWA_HEAD_DIM)
            outs["ks"].append(pos_major(k_new))
            outs["vs"].append(pos_major(v_new))
        wg, wp = ple_w_gate[i].astype(BF16), ple_w_proj[i].astype(BF16)
        g_mlp, g_ple = row2(norm_mlp[i]), row2(norm_ple[i])
        g_fin = row2(norm_final) if i == depth - 1 else None
        tf = _row_tile(w1_all.shape[2], 1024)
        xp = _mlp(xp, g_mlp, w1_all, w2_all, i, tmp, tf)
        xs = _mlp(xs, g_mlp, w1_all, w2_all, i, tms, tf)
        xp = _ple(xp, g_ple, wg, p_prompt.reshape(depth, mp, -1), i, wp, tmp, g_fin)
        xs = _ple(xs, g_ple, wg, p_sample.reshape(depth, ms, -1), i, wp, tms, g_fin)

    st = lambda k: jnp.stack(outs[k])
    return (xp.reshape(batch, seq, d), xs.reshape(bd, t_new, d),
            st("lat_p"), st("rope_p"), st("lat_s"), st("rope_s"),
            st("kp"), st("vp"), st("ks"), st("vs"))
```
